```python
import math
import jax, jax.numpy as jnp
from jax import lax
import numpy as np

D_MODEL = 1024
BATCH = 4
SEQ = 4096
DEPTH = 4

D_BRANCH = D_MODEL // 2
SSM_GROUP_SIZE = 16
SSM_GROUPS = D_BRANCH // SSM_GROUP_SIZE
SSM_STATE = 64
SB_HEAD_DIM = 64
SB_HEADS = D_BRANCH // SB_HEAD_DIM
DIFF_HEAD_DIM = 64
DIFF_HEADS = D_BRANCH // (2 * DIFF_HEAD_DIM)
N_BRANCHES = 3
Q_BLOCK = 128
ROPE_THETA = 10000.0
NORM_EPS = 1e-6
DT_MIN = 1e-3
DT_MAX = 1e-1
D_IN_PROJ = 10 * D_BRANCH + N_BRANCHES * D_MODEL

kernel_name = "hybrid_s5_stickbreak_diffattn_gated"


def _rmsnorm(x, w):
    xf = x.astype(jnp.float32)
    y = xf * lax.rsqrt(jnp.mean(xf * xf, axis=-1, keepdims=True) + NORM_EPS)
    return (y * w.astype(jnp.float32)).astype(x.dtype)


def _rope(x, positions):
    d = x.shape[-1]
    half = d // 2
    inv_freq = ROPE_THETA ** (-jnp.arange(half, dtype=jnp.float32) / half)
    ang = positions.astype(jnp.float32)[:, None] * inv_freq[None, :]
    bshape = (positions.shape[0],) + (1,) * (x.ndim - 3) + (half,)
    cos = jnp.cos(ang).reshape(bshape)
    sin = jnp.sin(ang).reshape(bshape)
    xf = x.astype(jnp.float32)
    x1, x2 = xf[..., :half], xf[..., half:]
    out = jnp.concatenate([x1 * cos - x2 * sin, x2 * cos + x1 * sin], axis=-1)
    return out.astype(x.dtype)


def _to_blocks(q):
    *lead, s, d = q.shape
    qb = q.reshape(*lead, s // Q_BLOCK, Q_BLOCK, d)
    return jnp.moveaxis(qb, -3, 0)


def _from_blocks(o):
    o = jnp.moveaxis(o, 0, -3)
    *lead, nb, qb, d = o.shape
    return o.reshape(*lead, nb * qb, d)


def _s5_scan(u, a_re, a_im, log_dt, b_re, b_im, c_re, c_im, d_skip):
    f32 = jnp.float32
    u = u.astype(f32)
    a_re, a_im = a_re.astype(f32), a_im.astype(f32)
    dt = jnp.exp(log_dt.astype(f32))[:, None]
    mag = jnp.exp(dt * a_re)
    abar_re = mag * jnp.cos(dt * a_im)
    abar_im = mag * jnp.sin(dt * a_im)
    denom = a_re * a_re + a_im * a_im
    coef_re = ((abar_re - 1.0) * a_re + abar_im * a_im) / denom
    coef_im = (abar_im * a_re - (abar_re - 1.0) * a_im) / denom
    b_re, b_im = b_re.astype(f32), b_im.astype(f32)
    bb_re = coef_re[..., None] * b_re - coef_im[..., None] * b_im
    bb_im = coef_re[..., None] * b_im + coef_im[..., None] * b_re
    bu_re = jnp.einsum('bsgc,gpc->bsgp', u, bb_re)
    bu_im = jnp.einsum('bsgc,gpc->bsgp', u, bb_im)
    ar = jnp.broadcast_to(abar_re, bu_re.shape)
    ai = jnp.broadcast_to(abar_im, bu_re.shape)

    def combine(e1, e2):
        a1r, a1i, b1r, b1i = e1
        a2r, a2i, b2r, b2i = e2
        return (a2r * a1r - a2i * a1i,
                a2r * a1i + a2i * a1r,
                a2r * b1r - a2i * b1i + b2r,
                a2r * b1i + a2i * b1r + b2i)

    _, _, xr, xi = lax.associative_scan(combine, (ar, ai, bu_re, bu_im), axis=1)
    y = (jnp.einsum('bsgp,gcp->bsgc', xr, c_re.astype(f32))
         - jnp.einsum('bsgp,gcp->bsgc', xi, c_im.astype(f32)))
    return y + d_skip.astype(f32) * u


def _stick_breaking(q, k, v):
    s_len, d = q.shape[2], q.shape[3]
    scale = 1.0 / math.sqrt(d)
    key_pos = jnp.arange(s_len, dtype=jnp.int32)

    def block(args):
        qb, start = args
        z = jnp.einsum('bhqd,bhkd->bhqk', qb, k).astype(jnp.float32) * scale
        q_pos = start + jnp.arange(Q_BLOCK, dtype=jnp.int32)
        mask = key_pos[None, :] < q_pos[:, None]
        log_keep = jnp.where(mask, jax.nn.log_sigmoid(-z), 0.0)
        after = lax.cumsum(log_keep, axis=3, reverse=True) - log_keep
        w = jnp.where(mask, jnp.exp(jax.nn.log_sigmoid(z) + after), 0.0)
        return jnp.einsum('bhqk,bhkd->bhqd', w.astype(v.dtype), v)

    starts = jnp.arange(s_len // Q_BLOCK, dtype=jnp.int32) * Q_BLOCK
    out = lax.map(block, (_to_blocks(q), starts))
    return _from_blocks(out)


def _diff_attention(q, k, v, lam):
    s_len, d = q.shape[3], q.shape[4]
    scale = 1.0 / math.sqrt(d)
    key_pos = jnp.arange(s_len, dtype=jnp.int32)

    def block(args):
        qb, start = args
        s = jnp.einsum('bhiqd,bhikd->bhiqk', qb, k).astype(jnp.float32) * scale
        q_pos = start + jnp.arange(Q_BLOCK, dtype=jnp.int32)
        mask = key_pos[None, :] <= q_pos[:, None]
        p = jax.nn.softmax(jnp.where(mask, s, -jnp.inf), axis=-1)
        a = p[:, :, 0] - lam * p[:, :, 1]
        return jnp.einsum('bhqk,bhkd->bhqd', a.astype(v.dtype), v)

    starts = jnp.arange(s_len // Q_BLOCK, dtype=jnp.int32) * Q_BLOCK
    out = lax.map(block, (_to_blocks(q), starts))
    return _from_blocks(out)


def _layer(x, layer_idx, norm_w, w_in, b_merge, a_re, a_im, log_dt, b_re, b_im,
           c_re, c_im, d_skip, w_glu, b_glu, lq1, lk1, lq2, lk2, subln_w,
           w_branch, w_out):
    bsz, s_len, _ = x.shape
    h = _rmsnorm(x, norm_w)
    proj = jnp.einsum('bsd,de->bse', h, w_in)
    (ssm_u, ssm_gate, sb_q, sb_k, sb_v, sb_gate,
     df_q, df_k, df_v, df_gate, merge_logits) = jnp.split(
        proj, [D_BRANCH * i for i in range(1, 11)], axis=-1)

    y = _s5_scan(ssm_u.reshape(bsz, s_len, SSM_GROUPS, SSM_GROUP_SIZE),
                 a_re, a_im, log_dt, b_re, b_im, c_re, c_im, d_skip)
    y = jax.nn.gelu(y.reshape(bsz, s_len, D_BRANCH).astype(x.dtype))
    y = y * jax.nn.sigmoid(y @ w_glu + b_glu)
    o_ssm = y * jax.nn.silu(ssm_gate)

    def sb_heads(t):
        return t.reshape(bsz, s_len, SB_HEADS, SB_HEAD_DIM).transpose(0, 2, 1, 3)
    o = _stick_breaking(sb_heads(sb_q), sb_heads(sb_k), sb_heads(sb_v))
    o_sb = o.transpose(0, 2, 1, 3).reshape(bsz, s_len, D_BRANCH) * jax.nn.silu(sb_gate)

    positions = jnp.arange(s_len, dtype=jnp.int32)
    q = _rope(df_q.reshape(bsz, s_len, DIFF_HEADS, 2, DIFF_HEAD_DIM), positions).transpose(0, 2, 3, 1, 4)
    k = _rope(df_k.reshape(bsz, s_len, DIFF_HEADS, 2, DIFF_HEAD_DIM), positions).transpose(0, 2, 3, 1, 4)
    v = df_v.reshape(bsz, s_len, DIFF_HEADS, 2 * DIFF_HEAD_DIM).transpose(0, 2, 1, 3)
    lam_init = 0.8 - 0.6 * math.exp(-0.3 * layer_idx)
    lam = (jnp.exp(jnp.sum(lq1.astype(jnp.float32) * lk1.astype(jnp.float32)))
           - jnp.exp(jnp.sum(lq2.astype(jnp.float32) * lk2.astype(jnp.float32))) + lam_init)
    o = _diff_attention(q, k, v, lam)
    o = _rmsnorm(o, subln_w) * (1.0 - lam_init)
    o_diff = o.transpose(0, 2, 1, 3).reshape(bsz, s_len, D_BRANCH) * jax.nn.silu(df_gate)

    branches = jnp.stack([o_ssm, o_sb, o_diff], axis=2)
    gates = jax.nn.sigmoid(merge_logits + b_merge).reshape(bsz, s_len, N_BRANCHES, D_MODEL)
    merged = jnp.einsum('bsnc,ncd->bsnd', branches, w_branch)
    merged = jnp.sum(gates * merged, axis=2)
    return x + merged @ w_out


def setup_inputs(seed: int = 0) -> dict:
    key = jax.random.key(seed)
    ks = jax.random.split(key, 24)
    f32 = jnp.float32
    n = jnp.arange(SSM_STATE, dtype=f32)
    g, p, c = SSM_GROUPS, SSM_STATE, SSM_GROUP_SIZE
    return {
        "x": jax.random.normal(ks[0], (BATCH, SEQ, D_MODEL), f32),
        "norm_w": 1.0 + 0.02 * jax.random.normal(ks[1], (DEPTH, D_MODEL), f32),
        "w_in": jax.random.normal(ks[2], (DEPTH, D_MODEL, D_IN_PROJ), f32) * D_MODEL ** -0.5,
        "b_merge": 0.01 * jax.random.normal(ks[3], (DEPTH, N_BRANCHES * D_MODEL), f32),
        "ssm_a_re": -0.5 + 0.01 * jax.random.normal(ks[4], (DEPTH, g, p), f32),
        "ssm_a_im": math.pi * n + 0.01 * jax.random.normal(ks[5], (DEPTH, g, p), f32),
        "ssm_log_dt": jax.random.uniform(ks[6], (DEPTH, g), f32,
                                         minval=math.log(DT_MIN), maxval=math.log(DT_MAX)),
        "ssm_b_re": jax.random.normal(ks[7], (DEPTH, g, p, c), f32) * (2 * c) ** -0.5,
        "ssm_b_im": jax.random.normal(ks[8], (DEPTH, g, p, c), f32) * (2 * c) ** -0.5,
        "ssm_c_re": jax.random.normal(ks[9], (DEPTH, g, c, p), f32) * p ** -0.5,
        "ssm_c_im": jax.random.normal(ks[10], (DEPTH, g, c, p), f32) * p ** -0.5,
        "ssm_d": jax.random.normal(ks[11], (DEPTH, g, c), f32),
        "ssm_w_glu": jax.random.normal(ks[12], (DEPTH, D_BRANCH, D_BRANCH), f32) * D_BRANCH ** -0.5,
        "ssm_b_glu": 0.01 * jax.random.normal(ks[13], (DEPTH, D_BRANCH), f32),
        "diff_lq1": 0.1 * jax.random.normal(ks[14], (DEPTH, DIFF_HEAD_DIM), f32),
        "diff_lk1": 0.1 * jax.random.normal(ks[15], (DEPTH, DIFF_HEAD_DIM), f32),
        "diff_lq2": 0.1 * jax.random.normal(ks[16], (DEPTH, DIFF_HEAD_DIM), f32),
        "diff_lk2": 0.1 * jax.random.normal(ks[17], (DEPTH, DIFF_HEAD_DIM), f32),
        "diff_subln_w": 1.0 + 0.02 * jax.random.normal(ks[18], (DEPTH, 2 * DIFF_HEAD_DIM), f32),
        "w_branch": jax.random.normal(ks[19], (DEPTH, N_BRANCHES, D_BRANCH, D_MODEL), f32) * D_BRANCH ** -0.5,
        "w_out": jax.random.normal(ks[20], (DEPTH, D_MODEL, D_MODEL), f32) * D_MODEL ** -0.5,
        "final_norm_w": 1.0 + 0.02 * jax.random.normal(ks[21], (D_MODEL,), f32),
    }


def reference(x, norm_w, w_in, b_merge, ssm_a_re, ssm_a_im, ssm_log_dt, ssm_b_re,
              ssm_b_im, ssm_c_re, ssm_c_im, ssm_d, ssm_w_glu, ssm_b_glu, diff_lq1,
              diff_lk1, diff_lq2, diff_lk2, diff_subln_w, w_branch, w_out, final_norm_w):
    for i in range(DEPTH):
        x = _layer(x, i, norm_w[i], w_in[i], b_merge[i], ssm_a_re[i], ssm_a_im[i],
                   ssm_log_dt[i], ssm_b_re[i], ssm_b_im[i], ssm_c_re[i], ssm_c_im[i],
                   ssm_d[i], ssm_w_glu[i], ssm_b_glu[i], diff_lq1[i], diff_lk1[i],
                   diff_lq2[i], diff_lk2[i], diff_subln_w[i], w_branch[i], w_out[i])
    return _rmsnorm(x, final_norm_w)
```

```python
import functools
import math

import jax
import jax.numpy as jnp
from jax import lax
from jax.experimental import pallas as pl
from jax.experimental.pallas import tpu as pltpu

F32 = jnp.float32
BF16 = jnp.bfloat16

NORM_EPS = 1e-6
ROPE_THETA = 10000.0
HEAD_DIM = 64
SSM_GROUP = 16
SSM_STATE = 64
SSM_CHUNK = 16
SUBLANES = 8
LANES = 128
ATT_TILE = 256
VMEM_LIMIT = 48 * 1024 * 1024


def _cparams(sem):
    return pltpu.CompilerParams(dimension_semantics=sem, vmem_limit_bytes=VMEM_LIMIT)


def _in_proj_kernel(x_ref, nw_ref, w_ref, cos_ref, sin_ref, o_ref, h_ref, *,
                    tn, sbq_tile, dfq_tile, dfk_tile, scale):
    j = pl.program_id(1)

    @pl.when(j == 0)
    def _():
        x = x_ref[...]
        ms = jnp.mean(x * x, axis=-1, keepdims=True)
        h_ref[...] = (x * lax.rsqrt(ms + NORM_EPS) * nw_ref[...]).astype(BF16)

    acc = jnp.dot(h_ref[...], w_ref[...], preferred_element_type=F32)
    is_rope = (j == dfq_tile) | (j == dfk_tile)

    @pl.when(is_rope)
    def _():
        lane = lax.broadcasted_iota(jnp.int32, acc.shape, 1)
        first_half = (lane % HEAD_DIM) < (HEAD_DIM // 2)
        swapped = jnp.where(first_half,
                            pltpu.roll(acc, tn - HEAD_DIM // 2, 1),
                            pltpu.roll(acc, HEAD_DIM // 2, 1))
        r = acc * cos_ref[...] + swapped * sin_ref[...]
        o_ref[...] = r * jnp.where(j == dfq_tile, scale, 1.0)

    @pl.when(j == sbq_tile)
    def _():
        o_ref[...] = acc * scale

    @pl.when(jnp.logical_not(is_rope | (j == sbq_tile)))
    def _():
        o_ref[...] = acc


def _in_proj(x2d, norm_w, w_bf16, cos_t, sin_t, seq, *, tm=512, tn=512):
    t, d = x2d.shape
    n = w_bf16.shape[1]
    d_branch = d // 2
    assert tn == d_branch and t % tm == 0 and seq % tm == 0 and n % tn == 0
    pos_tiles = seq // tm
    kern = functools.partial(_in_proj_kernel, tn=tn, sbq_tile=2, dfq_tile=6, dfk_tile=7,
                             scale=1.0 / math.sqrt(HEAD_DIM))
    return pl.pallas_call(
        kern,
        grid=(t // tm, n // tn),
        in_specs=[
            pl.BlockSpec((tm, d), lambda i, j: (i, 0)),
            pl.BlockSpec((1, d), lambda i, j: (0, 0)),
            pl.BlockSpec((d, tn), lambda i, j: (0, j)),
            pl.BlockSpec((tm, tn), lambda i, j: (i % pos_tiles, 0)),
            pl.BlockSpec((tm, tn), lambda i, j: (i % pos_tiles, 0)),
        ],
        out_specs=pl.BlockSpec((tm, tn), lambda i, j: (i, j)),
        out_shape=jax.ShapeDtypeStruct((t, n), F32),
        scratch_shapes=[pltpu.VMEM((tm, d), BF16)],
        compiler_params=_cparams(("arbitrary", "arbitrary")),
        name="in_proj",
    )(x2d, norm_w.reshape(1, d), w_bf16, cos_t, sin_t)


def _rope_tables(seq, width):
    half = HEAD_DIM // 2
    inv_freq = ROPE_THETA ** (-jnp.arange(half, dtype=F32) / half)
    ang = jnp.arange(seq, dtype=jnp.int32).astype(F32)[:, None] * inv_freq[None, :]
    cos, sin = jnp.cos(ang), jnp.sin(ang)
    reps = width // HEAD_DIM
    cos_t = jnp.tile(jnp.concatenate([cos, cos], axis=-1), (1, reps))
    sin_t = jnp.tile(jnp.concatenate([-sin, sin], axis=-1), (1, reps))
    return cos_t, sin_t


def _ssm_kernel(u_ref, tw_ref, v_ref, lr_ref, li_ref, d_ref, y_ref,
                sre_ref, sim_ref, xre_ref, xim_ref, *, n_chunks):
    width = SSM_CHUNK * SSM_GROUP
    u = u_ref[0]
    ys = jnp.dot(u.astype(BF16), tw_ref[0], preferred_element_type=F32)
    sre_ref[...] = ys[:, width:width + LANES]
    sim_ref[...] = ys[:, width + LANES:width + 2 * LANES]
    lr = jnp.broadcast_to(lr_ref[0], (SUBLANES, LANES))
    li = jnp.broadcast_to(li_ref[0], (SUBLANES, LANES))

    def step(k, carry):
        xr, xi = carry
        rows = pl.ds(pl.multiple_of(k * SUBLANES, SUBLANES), SUBLANES)
        xre_ref[rows, :] = xr
        xim_ref[rows, :] = xi
        nxr = lr * xr - li * xi + sre_ref[rows, :]
        nxi = lr * xi + li * xr + sim_ref[rows, :]
        return nxr, nxi

    zero = jnp.zeros((SUBLANES, LANES), F32)
    lax.fori_loop(0, n_chunks, step, (zero, zero))

    y = ys[:, :width] + d_ref[0] * u
    y = y + jnp.dot(xre_ref[...].astype(BF16), v_ref[0, :LANES, :], preferred_element_type=F32)
    y = y + jnp.dot(xim_ref[...].astype(BF16), v_ref[0, LANES:, :], preferred_element_type=F32)
    y_ref[0] = y


def _ssm_matrices(a_re, a_im, log_dt, b_re, b_im, c_re, c_im, d_skip):
    g, p, c = b_re.shape
    L = SSM_CHUNK
    dt = jnp.exp(log_dt.astype(F32))[:, None]
    mag = jnp.exp(dt * a_re)
    abar_re = mag * jnp.cos(dt * a_im)
    abar_im = mag * jnp.sin(dt * a_im)
    denom = a_re * a_re + a_im * a_im
    coef_re = ((abar_re - 1.0) * a_re + abar_im * a_im) / denom
    coef_im = (abar_im * a_re - (abar_re - 1.0) * a_im) / denom
    bb_re = coef_re[..., None] * b_re - coef_im[..., None] * b_im
    bb_im = coef_re[..., None] * b_im + coef_im[..., None] * b_re
    tau = jnp.arange(L + 1, dtype=F32)[:, None, None]
    pw_mag = jnp.exp(tau * (dt * a_re)[None])
    pw_re = pw_mag * jnp.cos(tau * (dt * a_im)[None])
    pw_im = pw_mag * jnp.sin(tau * (dt * a_im)[None])
    m_re = pw_re[..., None] * bb_re[None] - pw_im[..., None] * bb_im[None]
    m_im = pw_re[..., None] * bb_im[None] + pw_im[..., None] * bb_re[None]
    hi = lax.Precision.HIGHEST
    kern = (jnp.einsum('gcp,tgpd->tgcd', c_re, m_re[:L], precision=hi)
            - jnp.einsum('gcp,tgpd->tgcd', c_im, m_im[:L], precision=hi))
    s_idx = jnp.arange(L)[:, None]
    t_idx = jnp.arange(L)[None, :]
    lag = jnp.clip(t_idx - s_idx, 0, L - 1)
    toe = jnp.where((t_idx >= s_idx)[:, :, None, None, None], kern[lag], 0.0)
    toe = toe.transpose(2, 0, 4, 1, 3).reshape(g, L * c, L * c)
    rev = jnp.arange(L - 1, -1, -1)
    w_re = m_re[rev].transpose(1, 0, 3, 2).reshape(g, L * c, p)
    w_im = m_im[rev].transpose(1, 0, 3, 2).reshape(g, L * c, p)
    pad = jnp.zeros((g, L * c, LANES - p), F32)
    tw = jnp.concatenate([toe, w_re, pad, w_im, pad], axis=-1).astype(BF16)
    pr, pi = pw_re[1:], pw_im[1:]
    v_re = (c_re[None] * pr[:, :, None, :] - c_im[None] * pi[:, :, None, :])
    v_im = -(c_re[None] * pi[:, :, None, :] + c_im[None] * pr[:, :, None, :])
    v_re = v_re.transpose(1, 3, 0, 2).reshape(g, p, L * c)
    v_im = v_im.transpose(1, 3, 0, 2).reshape(g, p, L * c)
    vpad = jnp.zeros((g, LANES - p, L * c), F32)
    v = jnp.concatenate([v_re, vpad, v_im, vpad], axis=1).astype(BF16)
    lpad = jnp.zeros((g, LANES - p), F32)
    lam_re = jnp.concatenate([pw_re[L], lpad], axis=-1).reshape(g, 1, LANES)
    lam_im = jnp.concatenate([pw_im[L], lpad], axis=-1).reshape(g, 1, LANES)
    d_t = jnp.tile(d_skip.astype(F32), (1, L)).reshape(g, 1, L * c)
    return tw, v, lam_re, lam_im, d_t


def _ssm(u, mats):
    b, s, db = u.shape
    g = db // SSM_GROUP
    L = SSM_CHUNK
    nc = s // L
    assert b <= SUBLANES and s % L == 0
    width = L * SSM_GROUP
    tw, v, lam_re, lam_im, d_t = mats
    ug = u.reshape(b, nc, L, g, SSM_GROUP).transpose(3, 1, 0, 2, 4)
    ug = jnp.pad(ug, ((0, 0), (0, 0), (0, SUBLANES - b), (0, 0), (0, 0)))
    rows = nc * SUBLANES
    ug = ug.reshape(g, rows, width)
    kern = functools.partial(_ssm_kernel, n_chunks=nc)
    y = pl.pallas_call(
        kern,
        grid=(g,),
        in_specs=[
            pl.BlockSpec((1, rows, width), lambda i: (i, 0, 0)),
            pl.BlockSpec((1, width, width + 2 * LANES), lambda i: (i, 0, 0)),
            pl.BlockSpec((1, 2 * LANES, width), lambda i: (i, 0, 0)),
            pl.BlockSpec((1, 1, LANES), lambda i: (i, 0, 0)),
            pl.BlockSpec((1, 1, LANES), lambda i: (i, 0, 0)),
            pl.BlockSpec((1, 1, width), lambda i: (i, 0, 0)),
        ],
        out_specs=pl.BlockSpec((1, rows, width), lambda i: (i, 0, 0)),
        out_shape=jax.ShapeDtypeStruct((g, rows, width), F32),
        scratch_shapes=[pltpu.VMEM((rows, LANES), F32)] * 4,
        compiler_params=_cparams(("arbitrary",)),
        name="ssm_scan",
    )(ug, tw, v, lam_re, lam_im, d_t)
    y = y.reshape(g, nc, SUBLANES, L, SSM_GROUP)[:, :, :b]
    return y.transpose(2, 1, 3, 0, 4).reshape(b, s, db)


def _sb_kernel(q_ref, k_ref, vt_ref, o_ref, acc_ref, carry_ref):
    tk = tq = ATT_TILE
    seg = tk // SUBLANES
    qi = pl.program_id(2)
    q = q_ref[0, 0]
    row = lax.broadcasted_iota(jnp.int32, (tk, tq), 0)
    col = lax.broadcasted_iota(jnp.int32, (tk, tq), 1)
    key_local = (row % SUBLANES) * seg + row // SUBLANES
    diag_mask = key_local < col
    rid = lax.broadcasted_iota(jnp.int32, (SUBLANES, tq), 0)

    acc_ref[...] = jnp.zeros_like(acc_ref)
    carry_ref[...] = jnp.zeros_like(carry_ref)

    def block(kb, masked):
        kblk = k_ref[0, 0, kb]
        z = lax.dot_general(kblk, q, (((1,), (1,)), ((), ())), preferred_element_type=F32)
        l = jnp.log1p(jnp.exp(-jnp.abs(z)))
        ls_pos = jnp.minimum(z, 0.0) - l
        lk = ls_pos - z
        if masked:
            lk = jnp.where(diag_mask, lk, 0.0)
        run = jnp.zeros((SUBLANES, tq), F32)
        excl = [None] * seg
        for a in reversed(range(seg)):
            excl[a] = run
            run = run + lk[a * SUBLANES:(a + 1) * SUBLANES]
        offset = jnp.zeros((SUBLANES, tq), F32)
        for r in range(1, SUBLANES):
            offset = offset + jnp.where(rid < r, jnp.broadcast_to(run[r:r + 1], (SUBLANES, tq)), 0.0)
        base = offset + carry_ref[...]
        w = jnp.concatenate(
            [jnp.exp(ls_pos[a * SUBLANES:(a + 1) * SUBLANES] + (base + excl[a])) for a in range(seg)],
            axis=0)
        if masked:
            w = jnp.where(diag_mask, w, 0.0)
        acc_ref[...] += jnp.dot(vt_ref[0, 0, kb], w.astype(BF16), preferred_element_type=F32)
        carry_ref[...] += jnp.sum(run, axis=0, keepdims=True)

    block(qi, True)

    def body(it, _):
        block(qi - 1 - it, False)
        return 0

    lax.fori_loop(0, qi, body, 0)
    o_ref[0, 0] = acc_ref[...]


def _stick_breaking(q, k, v):
    b, s, hd = q.shape
    d = HEAD_DIM
    h = hd // d
    t = ATT_TILE
    nk = s // t
    seg = t // SUBLANES
    qh = q.astype(BF16).reshape(b, s, h, d).transpose(0, 2, 1, 3)
    kp = k.astype(BF16).reshape(b, nk, SUBLANES, seg, h, d).transpose(0, 4, 1, 3, 2, 5)
    kp = kp.reshape(b, h, nk, t, d)
    vt = v.astype(BF16).reshape(b, nk, SUBLANES, seg, h, d).transpose(0, 4, 1, 5, 3, 2)
    vt = vt.reshape(b, h, nk, d, t)
    ot = pl.pallas_call(
        _sb_kernel,
        grid=(b, h, nk),
        in_specs=[
            pl.BlockSpec((1, 1, t, d), lambda bi, hi, qi: (bi, hi, qi, 0)),
            pl.BlockSpec((1, 1, nk, t, d), lambda bi, hi, qi: (bi, hi, 0, 0, 0)),
            pl.BlockSpec((1, 1, nk, d, t), lambda bi, hi, qi: (bi, hi, 0, 0, 0)),
        ],
        out_specs=pl.BlockSpec((1, 1, d, t), lambda bi, hi, qi: (bi, hi, 0, qi)),
        out_shape=jax.ShapeDtypeStruct((b, h, d, s), F32),
        scratch_shapes=[pltpu.VMEM((d, t), F32), pltpu.VMEM((1, t), F32)],
        compiler_params=_cparams(("arbitrary", "arbitrary", "arbitrary")),
        name="stick_breaking",
    )(qh, kp, vt)
    return ot.transpose(0, 3, 1, 2).reshape(b, s, hd)


def _diff_kernel(q_ref, k_ref, vt_ref, lq1_ref, lk1_ref, lq2_ref, lk2_ref, sw_ref, o_ref,
                 acc_ref, m_ref, l_ref, *, lam_init):
    tk = tq = ATT_TILE
    qi = pl.program_id(2)
    row = lax.broadcasted_iota(jnp.int32, (tk, tq), 0)
    col = lax.broadcasted_iota(jnp.int32, (tk, tq), 1)
    diag_mask = row <= col

    acc_ref[...] = jnp.zeros_like(acc_ref)
    l_ref[...] = jnp.zeros_like(l_ref)
    m_ref[...] = jnp.full_like(m_ref, -jnp.inf)

    def block(kb, masked):
        vblk = vt_ref[0, 0, kb]
        for i in range(2):
            s = lax.dot_general(k_ref[0, 0, i, kb], q_ref[0, 0, i], (((1,), (1,)), ((), ())),
                                preferred_element_type=F32)
            if masked:
                s = jnp.where(diag_mask, s, -jnp.inf)
            m_prev = m_ref[i]
            m_new = jnp.maximum(m_prev, jnp.max(s, axis=0, keepdims=True))
            alpha = jnp.exp(m_prev - m_new)
            p = jnp.exp(s - m_new)
            l_ref[i] = alpha * l_ref[i] + jnp.sum(p, axis=0, keepdims=True)
            acc_ref[i] = alpha * acc_ref[i] + jnp.dot(vblk, p.astype(BF16), preferred_element_type=F32)
            m_ref[i] = m_new

    def body(kb, _):
        block(kb, False)
        return 0

    lax.fori_loop(0, qi, body, 0)
    block(qi, True)

    lam = (jnp.exp(jnp.sum(lq1_ref[...] * lk1_ref[...], axis=-1, keepdims=True))
           - jnp.exp(jnp.sum(lq2_ref[...] * lk2_ref[...], axis=-1, keepdims=True)) + lam_init)
    o = acc_ref[0] / l_ref[0] - lam * (acc_ref[1] / l_ref[1])
    ms = jnp.mean(o * o, axis=0, keepdims=True)
    o_ref[0, 0] = o * lax.rsqrt(ms + NORM_EPS) * sw_ref[...] * (1.0 - lam_init)


def _diff_attention(q, k, v, lq1, lk1, lq2, lk2, subln_w, lam_init):
    b, s, db = q.shape
    d = HEAD_DIM
    dv = 2 * d
    h = db // dv
    t = ATT_TILE
    nk = s // t
    qh = q.astype(BF16).reshape(b, s, h, 2, d).transpose(0, 2, 3, 1, 4)
    kh = k.astype(BF16).reshape(b, nk, t, h, 2, d).transpose(0, 3, 4, 1, 2, 5)
    vt = v.astype(BF16).reshape(b, nk, t, h, dv).transpose(0, 3, 1, 4, 2)
    sw = jnp.broadcast_to(subln_w.astype(F32)[:, None], (dv, t))
    vec = lambda a: a.astype(F32).reshape(1, d)
    kern = functools.partial(_diff_kernel, lam_init=lam_init)
    small = pl.BlockSpec((1, d), lambda bi, hi, qi: (0, 0))
    ot = pl.pallas_call(
        kern,
        grid=(b, h, nk),
        in_specs=[
            pl.BlockSpec((1, 1, 2, t, d), lambda bi, hi, qi: (bi, hi, 0, qi, 0)),
            pl.BlockSpec((1, 1, 2, nk, t, d), lambda bi, hi, qi: (bi, hi, 0, 0, 0, 0)),
            pl.BlockSpec((1, 1, nk, dv, t), lambda bi, hi, qi: (bi, hi, 0, 0, 0)),
            small, small, small, small,
            pl.BlockSpec((dv, t), lambda bi, hi, qi: (0, 0)),
        ],
        out_specs=pl.BlockSpec((1, 1, dv, t), lambda bi, hi, qi: (bi, hi, 0, qi)),
        out_shape=jax.ShapeDtypeStruct((b, h, dv, s), F32),
        scratch_shapes=[pltpu.VMEM((2, dv, t), F32), pltpu.VMEM((2, 1, t), F32),
                        pltpu.VMEM((2, 1, t), F32)],
        compiler_params=_cparams(("arbitrary", "arbitrary", "arbitrary")),
        name="diff_attention",
    )(qh, kh, vt, vec(lq1), vec(lk1), vec(lq2), vec(lk2), sw)
    return ot.transpose(0, 3, 1, 2).reshape(b, s, db)


def _gelu_tanh(x):
    return 0.5 * x * (1.0 + jnp.tanh(math.sqrt(2.0 / math.pi) * (x + 0.044715 * (x * x * x))))


def _silu(x):
    return x * jax.nn.sigmoid(x)


def _merge_kernel(x_ref, y_ref, osb_ref, odf_ref, g_ssm_ref, g_sb_ref, g_df_ref,
                  ml0_ref, ml1_ref, ml2_ref, wglu_ref, bglu_ref, bm_ref, wbr_ref, wout_ref,
                  fw_ref, o_ref, *, final_norm):
    d = x_ref.shape[-1]
    y = _gelu_tanh(y_ref[...])
    glu = jnp.dot(y.astype(BF16), wglu_ref[...], preferred_element_type=F32) + bglu_ref[...]
    branches = (y * jax.nn.sigmoid(glu) * _silu(g_ssm_ref[...]),
                osb_ref[...] * _silu(g_sb_ref[...]),
                odf_ref[...] * _silu(g_df_ref[...]))
    logits = (ml0_ref, ml1_ref, ml2_ref)
    merged = None
    for n in range(3):
        gate = jax.nn.sigmoid(logits[n][...] + bm_ref[:, n * d:(n + 1) * d])
        term = gate * jnp.dot(branches[n].astype(BF16), wbr_ref[n], preferred_element_type=F32)
        merged = term if merged is None else merged + term
    out = x_ref[...] + jnp.dot(merged.astype(BF16), wout_ref[...], preferred_element_type=F32)
    if final_norm:
        ms = jnp.mean(out * out, axis=-1, keepdims=True)
        out = out * lax.rsqrt(ms + NORM_EPS) * fw_ref[...]
    o_ref[...] = out


def _merge(x2d, y_ssm, o_sb, o_df, proj, w_glu, b_glu, b_merge, w_branch, w_out, final_w,
           final_norm, *, tm=256):
    t, d = x2d.shape
    db = d // 2
    assert t % tm == 0
    row = lambda i: (i, 0)
    const2 = lambda i: (0, 0)
    logits_tile0 = (10 * db) // d
    kern = functools.partial(_merge_kernel, final_norm=final_norm)
    return pl.pallas_call(
        kern,
        grid=(t // tm,),
        in_specs=[
            pl.BlockSpec((tm, d), row),
            pl.BlockSpec((tm, db), row),
            pl.BlockSpec((tm, db), row),
            pl.BlockSpec((tm, db), row),
            pl.BlockSpec((tm, db), lambda i: (i, 1)),
            pl.BlockSpec((tm, db), lambda i: (i, 5)),
            pl.BlockSpec((tm, db), lambda i: (i, 9)),
            pl.BlockSpec((tm, d), lambda i: (i, logits_tile0)),
            pl.BlockSpec((tm, d), lambda i: (i, logits_tile0 + 1)),
            pl.BlockSpec((tm, d), lambda i: (i, logits_tile0 + 2)),
            pl.BlockSpec((db, db), const2),
            pl.BlockSpec((1, db), const2),
            pl.BlockSpec((1, 3 * d), const2),
            pl.BlockSpec((3, db, d), lambda i: (0, 0, 0)),
            pl.BlockSpec((d, d), const2),
            pl.BlockSpec((1, d), const2),
        ],
        out_specs=pl.BlockSpec((tm, d), row),
        out_shape=jax.ShapeDtypeStruct((t, d), F32),
        compiler_params=_cparams(("arbitrary",)),
        name="merge_out",
    )(x2d, y_ssm, o_sb, o_df, proj, proj, proj, proj, proj, proj,
      w_glu.astype(BF16), b_glu.reshape(1, db).astype(F32), b_merge.reshape(1, 3 * d).astype(F32),
      w_branch.astype(BF16), w_out.astype(BF16), final_w.reshape(1, d).astype(F32))


def _layer(x, layer_idx, cos_t, sin_t, norm_w, w_in, b_merge, a_re, a_im, log_dt, b_re, b_im,
           c_re, c_im, d_skip, w_glu, b_glu, lq1, lk1, lq2, lk2, subln_w, w_branch, w_out,
           final_w, final_norm):
    b, s, d = x.shape
    db = d // 2
    x2d = x.reshape(b * s, d)
    proj = _in_proj(x2d, norm_w, w_in.astype(BF16), cos_t, sin_t, s)
    p3 = proj.reshape(b, s, -1)
    col = lambda i: p3[..., i * db:(i + 1) * db]

    mats = _ssm_matrices(a_re, a_im, log_dt, b_re, b_im, c_re, c_im, d_skip)
    y_ssm = _ssm(col(0), mats)
    o_sb = _stick_breaking(col(2), col(3), col(4))
    lam_init = 0.8 - 0.6 * math.exp(-0.3 * layer_idx)
    o_df = _diff_attention(col(6), col(7), col(8), lq1, lk1, lq2, lk2, subln_w, lam_init)

    out = _merge(x2d, y_ssm.reshape(b * s, db), o_sb.reshape(b * s, db), o_df.reshape(b * s, db),
                 proj, w_glu, b_glu, b_merge, w_branch, w_out, final_w, final_norm)
    return out.reshape(b, s, d)


def kernel(x, norm_w, w_in, b_merge, ssm_a_re, ssm_a_im, ssm_log_dt, ssm_b_re, ssm_b_im, ssm_c_re,
           ssm_c_im, ssm_d, ssm_w_glu, ssm_b_glu, diff_lq1, diff_lk1, diff_lq2, diff_lk2,
           diff_subln_w, w_branch, w_out, final_norm_w):
    depth = norm_w.shape[0]
    seq = x.shape[1]
    cos_t, sin_t = _rope_tables(seq, x.shape[2] // 2)
    for i in range(depth):
        x = _layer(x, i, cos_t, sin_t, norm_w[i], w_in[i], b_merge[i], ssm_a_re[i], ssm_a_im[i],
                   ssm_log_dt[i], ssm_b_re[i], ssm_b_im[i], ssm_c_re[i], ssm_c_im[i], ssm_d[i],
                   ssm_w_glu[i], ssm_b_glu[i], diff_lq1[i], diff_lk1[i], diff_lq2[i], diff_lk2[i],
                   diff_subln_w[i], w_branch[i], w_out[i], final_norm_w, i == depth - 1)
    return x
```

```python
import functools
import math

import jax
import jax.numpy as jnp
from jax import lax
from jax.experimental import pallas as pl
from jax.experimental.pallas import tpu as pltpu

F32 = jnp.float32
BF16 = jnp.bfloat16

NORM_EPS = 1e-6
ROPE_THETA = 10000.0
HEAD_DIM = 64
N_PROJ_TILES = 16
SSM_GROUP = 16
SSM_STATE = 64
SSM_CHUNK = 16
SUBLANES = 8
LANES = 128
ATT_TILE = 256
VMEM_LIMIT = 56 * 1024 * 1024
LOG2E = 1.4426950408889634
SB_EXIT_COST = 120.0

_NT = (((1,), (1,)), ((), ()))


def _cparams(sem):
    return pltpu.CompilerParams(dimension_semantics=sem, vmem_limit_bytes=VMEM_LIMIT)


def _in_proj_kernel(x_ref, nw_ref, wrest_ref, wu_ref, wqk_ref, wvt_ref, cos_ref, sin_ref,
                    rest_ref, uj_ref, sbq_ref, sbk_ref, dfq_ref, dfk_ref, sbvt_ref, dfvt_ref, *,
                    sb_scale, df_scale):
    t = ATT_TILE
    db = sbq_ref.shape[-1]
    x = x_ref[...]
    ms = jnp.mean(x * x, axis=-1, keepdims=True)
    h = (x * lax.rsqrt(ms + NORM_EPS) * nw_ref[...]).astype(BF16)

    r = lax.broadcasted_iota(jnp.int32, (t, t), 0)
    c = lax.broadcasted_iota(jnp.int32, (t, t), 1)
    perm = jnp.where((r % SUBLANES) * (t // SUBLANES) + r // SUBLANES == c, 1.0, 0.0).astype(BF16)
    hperm = jnp.dot(perm, h, preferred_element_type=F32).astype(BF16)

    n_rest = rest_ref.shape[-1]
    step = 3 * db
    for c0 in range(0, n_rest, step):
        rest_ref[:, c0:c0 + step] = jnp.dot(h, wrest_ref[:, c0:c0 + step], preferred_element_type=F32)

    u = jnp.dot(h, wu_ref[...], preferred_element_type=F32)
    for j in range(db // LANES):
        uj_ref[j] = u[:, j * LANES:(j + 1) * LANES].reshape(t // SSM_CHUNK, SSM_CHUNK, LANES)

    def rope(a):
        lane = lax.broadcasted_iota(jnp.int32, a.shape, 1)
        first_half = (lane % HEAD_DIM) < (HEAD_DIM // 2)
        swapped = jnp.where(first_half,
                            pltpu.roll(a, db - HEAD_DIM // 2, 1),
                            pltpu.roll(a, HEAD_DIM // 2, 1))
        return a * cos_ref[...] + swapped * sin_ref[...]

    sbq_ref[...] = (jnp.dot(h, wqk_ref[:, 0:db], preferred_element_type=F32) * sb_scale).astype(BF16)
    sbk_ref[...] = jnp.dot(hperm, wqk_ref[:, db:2 * db], preferred_element_type=F32).astype(BF16)
    dfq_ref[...] = (rope(jnp.dot(h, wqk_ref[:, 2 * db:3 * db], preferred_element_type=F32))
                    * df_scale).astype(BF16)
    dfk_ref[...] = rope(jnp.dot(h, wqk_ref[:, 3 * db:4 * db], preferred_element_type=F32)).astype(BF16)
    sbvt_ref[0, 0] = lax.dot_general(wvt_ref[0:db, :], hperm, _NT,
                                     preferred_element_type=F32).astype(BF16)
    dfvt_ref[0, 0] = lax.dot_general(wvt_ref[db:2 * db, :], h, _NT,
                                     preferred_element_type=F32).astype(BF16)


def _split_in_weights(w_in, db):
    col = lambda i: w_in[:, i * db:(i + 1) * db]
    w_rest = jnp.concatenate([w_in[:, 10 * db:], col(1), col(5), col(9)], axis=1).astype(BF16)
    w_u = col(0).astype(BF16)
    w_qk = jnp.concatenate([col(2), col(3), col(6), col(7)], axis=1).astype(BF16)
    w_vt = jnp.concatenate([col(4).T, col(8).T], axis=0).astype(BF16)
    return w_rest, w_u, w_qk, w_vt


def _in_proj(x2d, norm_w, w_in, cos_t, sin_t, batch, seq):
    t_rows, d = x2d.shape
    db = d // 2
    tm = ATT_TILE
    assert seq % tm == 0 and w_in.shape[1] == N_PROJ_TILES * db
    nk = seq // tm
    w_rest, w_u, w_qk, w_vt = _split_in_weights(w_in, db)
    n_rest = w_rest.shape[1]
    scale = 1.0 / math.sqrt(HEAD_DIM)
    kern = functools.partial(_in_proj_kernel, sb_scale=scale, df_scale=scale * LOG2E)
    row = lambda i: (i, 0)
    whole = pl.BlockSpec(memory_space=pltpu.VMEM)
    qk_sds = jax.ShapeDtypeStruct((t_rows, db), BF16)
    vt_sds = jax.ShapeDtypeStruct((batch, nk, db, tm), BF16)
    vt_spec = pl.BlockSpec((1, 1, db, tm), lambda i: (i // nk, i % nk, 0, 0))
    nj = db // LANES
    chunks = tm // SSM_CHUNK
    uj_sds = jax.ShapeDtypeStruct((nj, t_rows // SSM_CHUNK, SSM_CHUNK, LANES), F32)
    uj_spec = pl.BlockSpec((nj, chunks, SSM_CHUNK, LANES), lambda i: (0, i, 0, 0))
    return pl.pallas_call(
        kern,
        grid=(t_rows // tm,),
        in_specs=[
            pl.BlockSpec((tm, d), row),
            pl.BlockSpec((1, d), lambda i: (0, 0)),
            whole, whole, whole, whole,
            pl.BlockSpec((tm, db), lambda i: (i % nk, 0)),
            pl.BlockSpec((tm, db), lambda i: (i % nk, 0)),
        ],
        out_specs=[
            pl.BlockSpec((tm, n_rest), row), uj_spec,
            pl.BlockSpec((tm, db), row), pl.BlockSpec((tm, db), row),
            pl.BlockSpec((tm, db), row), pl.BlockSpec((tm, db), row),
            vt_spec, vt_spec,
        ],
        out_shape=[jax.ShapeDtypeStruct((t_rows, n_rest), F32), uj_sds, qk_sds, qk_sds, qk_sds, qk_sds,
                   vt_sds, vt_sds],
        compiler_params=_cparams(("arbitrary",)),
        name="in_proj",
    )(x2d, norm_w.reshape(1, d), w_rest, w_u, w_qk, w_vt, cos_t, sin_t)


def _rope_tables(seq, width):
    half = HEAD_DIM // 2
    inv_freq = ROPE_THETA ** (-jnp.arange(half, dtype=F32) / half)
    ang = jnp.arange(seq, dtype=jnp.int32).astype(F32)[:, None] * inv_freq[None, :]
    cos, sin = jnp.cos(ang), jnp.sin(ang)
    reps = width // HEAD_DIM
    cos_t = jnp.tile(jnp.concatenate([cos, cos], axis=-1), (1, reps))
    sin_t = jnp.tile(jnp.concatenate([-sin, sin], axis=-1), (1, reps))
    return cos_t, sin_t


SSM_TILE_GROUPS = LANES // SSM_GROUP
SSM_TILE_STATES = SSM_TILE_GROUPS * SSM_STATE


def _expand_groups(a, rows_per_group, cols_per_group, col_groups):
    tg = SSM_TILE_GROUPS
    full = jnp.concatenate([a] * tg, axis=0)
    row = lax.broadcasted_iota(jnp.int32, full.shape, 0)
    col = lax.broadcasted_iota(jnp.int32, full.shape, 1)
    same = row // rows_per_group == (col // cols_per_group) % col_groups
    return jnp.where(same, full, 0.0).astype(BF16)


def _ssm_kernel(u_ref, kc_ref, wre_ref, wim_ref, vre_ref, vim_ref, lam_ref, d_ref, y_ref,
                t_ref, w_ref, v_ref, s_ref, x_ref, *, n_chunks):
    ns = SSM_TILE_STATES
    L = SSM_CHUNK
    tg = SSM_TILE_GROUPS

    @pl.when(pl.program_id(1) == 0)
    def _():
        t_ref[...] = jnp.zeros_like(t_ref)
        for lag in range(L):
            blk = _expand_groups(kc_ref[0, lag], SSM_GROUP, SSM_GROUP, tg)
            for s in range(L - lag):
                t_ref[s * LANES:(s + 1) * LANES, (s + lag) * LANES:(s + lag + 1) * LANES] = blk
        for s in range(L):
            rows = slice(s * LANES, (s + 1) * LANES)
            w_ref[rows, :ns] = _expand_groups(wre_ref[0, L - 1 - s], SSM_GROUP, SSM_STATE, tg)
            w_ref[rows, ns:] = _expand_groups(wim_ref[0, L - 1 - s], SSM_GROUP, SSM_STATE, tg)
        v_ref[:ns, :] = _expand_groups(vre_ref[0], SSM_STATE, SSM_GROUP, tg)
        v_ref[ns:, :] = _expand_groups(vim_ref[0], SSM_STATE, SSM_GROUP, tg)

    u = u_ref[0]
    ub = u.astype(BF16)
    s_ref[...] = jnp.dot(ub, w_ref[...], preferred_element_type=F32)
    lam_r = lam_ref[0, :, :ns]
    lam_i = lam_ref[0, :, ns:]

    def step(k, carry):
        xr, xi = carry
        row = pl.ds(k, 1)
        x_ref[row, :ns] = xr
        x_ref[row, ns:] = xi
        nxr = lam_r * xr - lam_i * xi + s_ref[row, :ns]
        nxi = lam_r * xi + lam_i * xr + s_ref[row, ns:]
        return nxr, nxi

    zero = jnp.zeros((1, ns), F32)
    lax.fori_loop(0, n_chunks, step, (zero, zero))

    y = jnp.dot(ub, t_ref[...], preferred_element_type=F32) + d_ref[0] * u
    y_ref[0] = y + jnp.dot(x_ref[...].astype(BF16), v_ref[...], preferred_element_type=F32)


def _ssm_matrices(a_re, a_im, log_dt, b_re, b_im, c_re, c_im, d_skip):
    g, p, c = b_re.shape
    L = SSM_CHUNK
    tg = SSM_TILE_GROUPS
    nj = g // tg
    dt = jnp.exp(log_dt.astype(F32))[:, None]
    mag = jnp.exp(dt * a_re)
    abar_re = mag * jnp.cos(dt * a_im)
    abar_im = mag * jnp.sin(dt * a_im)
    denom = a_re * a_re + a_im * a_im
    coef_re = ((abar_re - 1.0) * a_re + abar_im * a_im) / denom
    coef_im = (abar_im * a_re - (abar_re - 1.0) * a_im) / denom
    bb_re = coef_re[..., None] * b_re - coef_im[..., None] * b_im
    bb_im = coef_re[..., None] * b_im + coef_im[..., None] * b_re
    tau = jnp.arange(L + 1, dtype=F32)[:, None, None]
    pw_mag = jnp.exp(tau * (dt * a_re)[None])
    pw_re = pw_mag * jnp.cos(tau * (dt * a_im)[None])
    pw_im = pw_mag * jnp.sin(tau * (dt * a_im)[None])
    m_re = pw_re[..., None] * bb_re[None] - pw_im[..., None] * bb_im[None]
    m_im = pw_re[..., None] * bb_im[None] + pw_im[..., None] * bb_re[None]
    hi = lax.Precision.HIGHEST
    kern = (jnp.einsum('gcp,tgpd->tgcd', c_re, m_re[:L], precision=hi)
            - jnp.einsum('gcp,tgpd->tgcd', c_im, m_im[:L], precision=hi))
    kc = kern.transpose(0, 3, 1, 2).reshape(L, c, nj, tg * c).transpose(2, 0, 1, 3)

    def state_in(m):
        return m[:L].transpose(0, 3, 1, 2).reshape(L, c, nj, tg * p).transpose(2, 0, 1, 3)

    pr, pi = pw_re[1:], pw_im[1:]
    v_re = (c_re[None] * pr[:, :, None, :] - c_im[None] * pi[:, :, None, :])
    v_im = -(c_re[None] * pi[:, :, None, :] + c_im[None] * pr[:, :, None, :])

    def state_out(v):
        return v.transpose(3, 0, 1, 2).reshape(p, L, nj, tg * c).transpose(2, 0, 1, 3).reshape(
            nj, p, L * LANES)

    lam = jnp.concatenate([pw_re[L].reshape(nj, 1, tg * p), pw_im[L].reshape(nj, 1, tg * p)], axis=-1)
    d_t = jnp.broadcast_to(d_skip.astype(F32).reshape(nj, 1, 1, tg * c), (nj, 1, L, tg * c))
    return (kc, state_in(m_re), state_in(m_im), state_out(v_re), state_out(v_im), lam,
            d_t.reshape(nj, 1, L * LANES))


def _ssm(uj, mats, batch, seq):
    nj, rows, L, lanes = uj.shape
    assert L == SSM_CHUNK and lanes == LANES and seq % L == 0
    nc = seq // L
    width = L * LANES
    ns2 = 2 * SSM_TILE_STATES
    kc, w_re, w_im, v_re, v_im, lam, d_t = mats
    kern = functools.partial(_ssm_kernel, n_chunks=nc)

    def per_tile(a):
        nd = a.ndim - 1
        return pl.BlockSpec((1,) + a.shape[1:], lambda j, b: (j,) + (0,) * nd)

    act = pl.BlockSpec((1, nc, width), lambda j, b: (j, b, 0))
    params = (kc, w_re, w_im, v_re, v_im, lam, d_t)
    y = pl.pallas_call(
        kern,
        grid=(nj, batch),
        in_specs=[act] + [per_tile(a) for a in params],
        out_specs=act,
        out_shape=jax.ShapeDtypeStruct((nj, rows, width), F32),
        scratch_shapes=[pltpu.VMEM((width, width), BF16), pltpu.VMEM((width, ns2), BF16),
                        pltpu.VMEM((ns2, width), BF16),
                        pltpu.VMEM((nc, ns2), F32), pltpu.VMEM((nc, ns2), F32)],
        compiler_params=_cparams(("arbitrary", "arbitrary")),
        name="ssm_scan",
    )(uj.reshape(nj, rows, width), *params)
    return y.reshape(nj, rows, L, LANES)


def _split_halves(q_ref):
    qf = q_ref[0].astype(F32)
    lane = lax.broadcasted_iota(jnp.int32, qf.shape, 1)
    return [jnp.where((lane // HEAD_DIM) == i, qf, 0.0).astype(BF16) for i in range(2)]


def _att_specs(t, nk):
    q_spec = pl.BlockSpec((1, t, LANES), lambda bi, pi, qi: (bi, qi, pi))
    k_spec = pl.BlockSpec((1, nk, t, LANES), lambda bi, pi, qi: (bi, 0, 0, pi))
    vt_spec = pl.BlockSpec((1, nk, LANES, t), lambda bi, pi, qi: (bi, 0, pi, 0))
    return q_spec, k_spec, vt_spec


def _sb_kernel(q_ref, k_ref, vt_ref, o_ref, z_ref, w_ref, acc_ref, carry_ref):
    tk = tq = ATT_TILE
    d = HEAD_DIM
    seg = tk // SUBLANES
    qi = pl.program_id(2)
    qs = _split_halves(q_ref)
    row = lax.broadcasted_iota(jnp.int32, (tk, tq), 0)
    col = lax.broadcasted_iota(jnp.int32, (tk, tq), 1)
    diag_mask = (row % SUBLANES) * seg + row // SUBLANES < col
    rid = lax.broadcasted_iota(jnp.int32, (SUBLANES, tq), 0)

    def scores(kb, hh):
        return lax.dot_general(k_ref[0, kb], qs[hh], _NT, preferred_element_type=F32)

    def weights(z, carry, masked):
        cost = jnp.maximum(z, 0.0) + jnp.log(1.0 + jnp.exp(-jnp.abs(z)))
        if masked:
            cost = jnp.where(diag_mask, cost, 0.0)
        run = jnp.zeros((SUBLANES, tq), F32)
        incl = [None] * seg
        for a in reversed(range(seg)):
            run = run + cost[a * SUBLANES:(a + 1) * SUBLANES]
            incl[a] = run
        offset = jnp.zeros((SUBLANES, tq), F32)
        for r in range(1, SUBLANES):
            offset = offset + jnp.where(rid < r, jnp.broadcast_to(run[r:r + 1], (SUBLANES, tq)), 0.0)
        base = offset if carry is None else offset + carry
        w = jnp.concatenate(
            [jnp.exp((z[a * SUBLANES:(a + 1) * SUBLANES] - incl[a]) - base) for a in range(seg)],
            axis=0)
        if masked:
            w = jnp.where(diag_mask, w, 0.0)
        return w.astype(BF16), jnp.sum(run, axis=0, keepdims=True)

    first_prev = jnp.maximum(qi - 1, 0)
    for hh in range(2):
        w, tot = weights(scores(qi, hh), None, True)
        w_ref[hh] = w
        carry_ref[hh] = tot
        z_ref[hh] = scores(first_prev, hh)
    acc_ref[...] = jnp.zeros_like(acc_ref)

    def not_done():
        return jnp.min(carry_ref[...]) < SB_EXIT_COST

    def cond(state):
        it, go = state
        return jnp.logical_and(it < qi, go)

    def body(state):
        it, _ = state
        kb = qi - 1 - it
        nxt = jnp.maximum(kb - 1, 0)
        for hh in range(2):
            rows = slice(hh * d, (hh + 1) * d)
            pv = jnp.dot(vt_ref[0, kb + 1, rows, :], w_ref[hh], preferred_element_type=F32)
            z_next = scores(nxt, hh)
            w, tot = weights(z_ref[hh], carry_ref[hh], False)
            acc_ref[rows, :] += pv
            w_ref[hh] = w
            z_ref[hh] = z_next
            carry_ref[hh] += tot
        return it + 1, not_done()

    n_done, _ = lax.while_loop(cond, body, (jnp.int32(0), not_done()))
    last = qi - n_done
    for hh in range(2):
        rows = slice(hh * d, (hh + 1) * d)
        acc_ref[rows, :] += jnp.dot(vt_ref[0, last, rows, :], w_ref[hh], preferred_element_type=F32)
    o_ref[0] = acc_ref[...].T


def _stick_breaking(q, k, vt, batch, seq):
    t = ATT_TILE
    nk = seq // t
    db = q.shape[-1]
    q_spec, k_spec, vt_spec = _att_specs(t, nk)
    return pl.pallas_call(
        _sb_kernel,
        grid=(batch, db // LANES, nk),
        in_specs=[q_spec, k_spec, vt_spec],
        out_specs=q_spec,
        out_shape=jax.ShapeDtypeStruct((batch, seq, db), F32),
        scratch_shapes=[pltpu.VMEM((2, t, t), F32), pltpu.VMEM((2, t, t), BF16),
                        pltpu.VMEM((LANES, t), F32), pltpu.VMEM((2, 1, t), F32)],
        compiler_params=_cparams(("arbitrary", "arbitrary", "arbitrary")),
        name="stick_breaking",
    )(q.reshape(batch, seq, db), k.reshape(batch, nk, t, db), vt)


def _diff_kernel(q_ref, k_ref, vt_ref, lq1_ref, lk1_ref, lq2_ref, lk2_ref, sw_ref, o_ref,
                 s_ref, p_ref, acc_ref, m_ref, l_ref, *, lam_init):
    tk = tq = ATT_TILE
    qi = pl.program_id(2)
    qs = _split_halves(q_ref)
    row = lax.broadcasted_iota(jnp.int32, (tk, tq), 0)
    col = lax.broadcasted_iota(jnp.int32, (tk, tq), 1)
    diag_mask = row <= col

    def scores(kb, i):
        return lax.dot_general(k_ref[0, kb], qs[i], _NT, preferred_element_type=F32)

    first_prev = jnp.maximum(qi - 1, 0)
    for i in range(2):
        s = jnp.where(diag_mask, scores(qi, i), -jnp.inf)
        m = jnp.max(s, axis=0, keepdims=True)
        p = jnp.exp2(s - m)
        m_ref[i] = m
        l_ref[i] = jnp.sum(p, axis=0, keepdims=True)
        p_ref[i] = p.astype(BF16)
        s_ref[i] = scores(first_prev, i)
    acc_ref[...] = jnp.zeros_like(acc_ref)

    def body(it, _):
        kb = qi - 1 - it
        nxt = jnp.maximum(kb - 1, 0)
        vblk = vt_ref[0, kb + 1]
        for i in range(2):
            pv = jnp.dot(vblk, p_ref[i], preferred_element_type=F32)
            s_next = scores(nxt, i)
            s = s_ref[i]
            m_prev = m_ref[i]
            m_new = jnp.maximum(m_prev, jnp.max(s, axis=0, keepdims=True))
            alpha = jnp.exp2(m_prev - m_new)
            p = jnp.exp2(s - m_new)
            l_ref[i] = alpha * l_ref[i] + jnp.sum(p, axis=0, keepdims=True)
            acc_ref[i] = alpha * (acc_ref[i] + pv)
            m_ref[i] = m_new
            p_ref[i] = p.astype(BF16)
            s_ref[i] = s_next
        return 0

    lax.fori_loop(0, qi, body, 0)
    vblk = vt_ref[0, 0]
    outs = [(acc_ref[i] + jnp.dot(vblk, p_ref[i], preferred_element_type=F32)) / l_ref[i]
            for i in range(2)]

    lam = (jnp.exp(jnp.sum(lq1_ref[...] * lk1_ref[...], axis=-1, keepdims=True))
           - jnp.exp(jnp.sum(lq2_ref[...] * lk2_ref[...], axis=-1, keepdims=True)) + lam_init)
    o = outs[0] - lam * outs[1]
    ms = jnp.mean(o * o, axis=0, keepdims=True)
    o_ref[0] = (o * lax.rsqrt(ms + NORM_EPS) * sw_ref[...] * (1.0 - lam_init)).T


def _diff_attention(q, k, vt, lq1, lk1, lq2, lk2, subln_w, lam_init, batch, seq):
    t = ATT_TILE
    nk = seq // t
    db = q.shape[-1]
    d = HEAD_DIM
    dv = 2 * d
    assert dv == LANES
    sw = jnp.broadcast_to(subln_w.astype(F32)[:, None], (dv, t))
    vec = lambda a: a.astype(F32).reshape(1, d)
    kern = functools.partial(_diff_kernel, lam_init=lam_init)
    small = pl.BlockSpec((1, d), lambda bi, hi, qi: (0, 0))
    q_spec, k_spec, vt_spec = _att_specs(t, nk)
    return pl.pallas_call(
        kern,
        grid=(batch, db // dv, nk),
        in_specs=[q_spec, k_spec, vt_spec, small, small, small, small,
                  pl.BlockSpec((dv, t), lambda bi, hi, qi: (0, 0))],
        out_specs=q_spec,
        out_shape=jax.ShapeDtypeStruct((batch, seq, db), F32),
        scratch_shapes=[pltpu.VMEM((2, t, t), F32), pltpu.VMEM((2, t, t), BF16),
                        pltpu.VMEM((2, dv, t), F32), pltpu.VMEM((2, 1, t), F32),
                        pltpu.VMEM((2, 1, t), F32)],
        compiler_params=_cparams(("arbitrary", "arbitrary", "arbitrary")),
        name="diff_attention",
    )(q.reshape(batch, seq, db), k.reshape(batch, nk, t, db), vt,
      vec(lq1), vec(lk1), vec(lq2), vec(lk2), sw)


def _gelu_tanh(x):
    return 0.5 * x * (1.0 + jnp.tanh(math.sqrt(2.0 / math.pi) * (x + 0.044715 * (x * x * x))))


def _silu(x):
    return x * jax.nn.sigmoid(x)


def _merge_kernel(x_ref, y0_ref, y1_ref, y2_ref, y3_ref, osb_ref, odf_ref, g_ssm_ref, g_sb_ref,
                  g_df_ref, ml0_ref, ml1_ref, ml2_ref, wglu_ref, bglu_ref, bm_ref, wbr_ref, wout_ref,
                  fw_ref, o_ref, *, final_norm):
    d = x_ref.shape[-1]
    tm = x_ref.shape[0]
    y = jnp.concatenate([r[0].reshape(tm, LANES) for r in (y0_ref, y1_ref, y2_ref, y3_ref)], axis=1)
    y = _gelu_tanh(y)
    glu = jnp.dot(y.astype(BF16), wglu_ref[...], preferred_element_type=F32) + bglu_ref[...]
    branches = (y * jax.nn.sigmoid(glu) * _silu(g_ssm_ref[...]),
                osb_ref[...] * _silu(g_sb_ref[...]),
                odf_ref[...] * _silu(g_df_ref[...]))
    logits = (ml0_ref, ml1_ref, ml2_ref)
    merged = None
    for n in range(3):
        gate = jax.nn.sigmoid(logits[n][...] + bm_ref[:, n * d:(n + 1) * d])
        term = gate * jnp.dot(branches[n].astype(BF16), wbr_ref[n], preferred_element_type=F32)
        merged = term if merged is None else merged + term
    out = x_ref[...] + jnp.dot(merged.astype(BF16), wout_ref[...], preferred_element_type=F32)
    if final_norm:
        ms = jnp.mean(out * out, axis=-1, keepdims=True)
        out = out * lax.rsqrt(ms + NORM_EPS) * fw_ref[...]
    o_ref[...] = out


def _merge(x2d, yj, o_sb, o_df, rest, w_glu, b_glu, b_merge, w_branch, w_out, final_w,
           final_norm, *, tm=256):
    t, d = x2d.shape
    db = d // 2
    assert t % tm == 0 and yj.shape[0] == 4
    row = lambda i: (i, 0)
    const2 = lambda i: (0, 0)
    gate_tile0 = (3 * d) // db
    chunks = tm // SSM_CHUNK
    y_spec = lambda j: pl.BlockSpec((1, chunks, SSM_CHUNK, LANES), lambda i: (j, i, 0, 0))
    kern = functools.partial(_merge_kernel, final_norm=final_norm)
    return pl.pallas_call(
        kern,
        grid=(t // tm,),
        in_specs=[
            pl.BlockSpec((tm, d), row),
            y_spec(0), y_spec(1), y_spec(2), y_spec(3),
            pl.BlockSpec((tm, db), row),
            pl.BlockSpec((tm, db), row),
            pl.BlockSpec((tm, db), lambda i: (i, gate_tile0)),
            pl.BlockSpec((tm, db), lambda i: (i, gate_tile0 + 1)),
            pl.BlockSpec((tm, db), lambda i: (i, gate_tile0 + 2)),
            pl.BlockSpec((tm, d), lambda i: (i, 0)),
            pl.BlockSpec((tm, d), lambda i: (i, 1)),
            pl.BlockSpec((tm, d), lambda i: (i, 2)),
            pl.BlockSpec((db, db), const2),
            pl.BlockSpec((1, db), const2),
            pl.BlockSpec((1, 3 * d), const2),
            pl.BlockSpec((3, db, d), lambda i: (0, 0, 0)),
            pl.BlockSpec((d, d), const2),
            pl.BlockSpec((1, d), const2),
        ],
        out_specs=pl.BlockSpec((tm, d), row),
        out_shape=jax.ShapeDtypeStruct((t, d), F32),
        compiler_params=_cparams(("arbitrary",)),
        name="merge_out",
    )(x2d, yj, yj, yj, yj, o_sb, o_df, rest, rest, rest, rest, rest, rest,
      w_glu.astype(BF16), b_glu.reshape(1, db).astype(F32), b_merge.reshape(1, 3 * d).astype(F32),
      w_branch.astype(BF16), w_out.astype(BF16), final_w.reshape(1, d).astype(F32))


def _layer(x, layer_idx, cos_t, sin_t, norm_w, w_in, b_merge, a_re, a_im, log_dt, b_re, b_im,
           c_re, c_im, d_skip, w_glu, b_glu, lq1, lk1, lq2, lk2, subln_w, w_branch, w_out,
           final_w, final_norm):
    b, s, d = x.shape
    db = d // 2
    x2d = x.reshape(b * s, d)
    rest, uj, sbq, sbk, dfq, dfk, sbvt, dfvt = _in_proj(x2d, norm_w, w_in, cos_t, sin_t, b, s)

    mats = _ssm_matrices(a_re, a_im, log_dt, b_re, b_im, c_re, c_im, d_skip)
    yj = _ssm(uj, mats, b, s)
    o_sb = _stick_breaking(sbq, sbk, sbvt, b, s)
    lam_init = 0.8 - 0.6 * math.exp(-0.3 * layer_idx)
    o_df = _diff_attention(dfq, dfk, dfvt, lq1, lk1, lq2, lk2, subln_w, lam_init, b, s)

    out = _merge(x2d, yj, o_sb.reshape(b * s, db), o_df.reshape(b * s, db),
                 rest, w_glu, b_glu, b_merge, w_branch, w_out, final_w, final_norm)
    return out.reshape(b, s, d)


def kernel(x, norm_w, w_in, b_merge, ssm_a_re, ssm_a_im, ssm_log_dt, ssm_b_re, ssm_b_im, ssm_c_re,
           ssm_c_im, ssm_d, ssm_w_glu, ssm_b_glu, diff_lq1, diff_lk1, diff_lq2, diff_lk2,
           diff_subln_w, w_branch, w_out, final_norm_w):
    depth = norm_w.shape[0]
    seq = x.shape[1]
    cos_t, sin_t = _rope_tables(seq, x.shape[2] // 2)
    for i in range(depth):
        x = _layer(x, i, cos_t, sin_t, norm_w[i], w_in[i], b_merge[i], ssm_a_re[i], ssm_a_im[i],
                   ssm_log_dt[i], ssm_b_re[i], ssm_b_im[i], ssm_c_re[i], ssm_c_im[i], ssm_d[i],
                   ssm_w_glu[i], ssm_b_glu[i], diff_lq1[i], diff_lk1[i], diff_lq2[i], diff_lk2[i],
                   diff_subln_w[i], w_branch[i], w_out[i], final_norm_w, i == depth - 1)
    return x
```

```python
import functools
import math

import jax
import jax.numpy as jnp
from jax import lax
from jax.experimental import pallas as pl
from jax.experimental.pallas import tpu as pltpu

F32 = jnp.float32
BF16 = jnp.bfloat16

NORM_EPS = 1e-6
ROPE_THETA = 10000.0
HEAD_DIM = 64
N_PROJ_TILES = 16
SSM_GROUP = 16
SSM_STATE = 64
SSM_CHUNK = 16
SUBLANES = 8
LANES = 128
ATT_TILE = 256
DIFF_Q_TILE = 512
VMEM_LIMIT = 56 * 1024 * 1024
LOG2E = 1.4426950408889634
SB_EXIT_COST = 120.0

_NT = (((1,), (1,)), ((), ()))


def _cparams(sem):
    return pltpu.CompilerParams(dimension_semantics=sem, vmem_limit_bytes=VMEM_LIMIT)


def _in_proj_kernel(x_ref, nw_ref, wrest_ref, wu_ref, wqk_ref, wvt_ref, cos_ref, sin_ref,
                    rest_ref, uj_ref, sbq_ref, sbk_ref, dfq_ref, dfk_ref, sbvt_ref, dfvt_ref, *,
                    sb_scale, df_scale):
    t = ATT_TILE
    db = sbq_ref.shape[-1]
    x = x_ref[...]
    ms = jnp.mean(x * x, axis=-1, keepdims=True)
    h = (x * lax.rsqrt(ms + NORM_EPS) * nw_ref[...]).astype(BF16)

    r = lax.broadcasted_iota(jnp.int32, (t, t), 0)
    c = lax.broadcasted_iota(jnp.int32, (t, t), 1)
    perm = jnp.where((r % SUBLANES) * (t // SUBLANES) + r // SUBLANES == c, 1.0, 0.0).astype(BF16)
    hperm = jnp.dot(perm, h, preferred_element_type=F32).astype(BF16)

    n_rest = rest_ref.shape[-1]
    step = 3 * db
    for c0 in range(0, n_rest, step):
        rest_ref[:, c0:c0 + step] = jnp.dot(h, wrest_ref[:, c0:c0 + step],
                                            preferred_element_type=F32).astype(rest_ref.dtype)

    u = jnp.dot(h, wu_ref[...], preferred_element_type=F32)
    for j in range(db // LANES):
        uj_ref[j] = u[:, j * LANES:(j + 1) * LANES].reshape(t // SSM_CHUNK, SSM_CHUNK, LANES)

    def rope(a):
        lane = lax.broadcasted_iota(jnp.int32, a.shape, 1)
        first_half = (lane % HEAD_DIM) < (HEAD_DIM // 2)
        swapped = jnp.where(first_half,
                            pltpu.roll(a, db - HEAD_DIM // 2, 1),
                            pltpu.roll(a, HEAD_DIM // 2, 1))
        return a * cos_ref[...] + swapped * sin_ref[...]

    sbq_ref[...] = (jnp.dot(h, wqk_ref[:, 0:db], preferred_element_type=F32) * sb_scale).astype(BF16)
    sbk_ref[...] = jnp.dot(hperm, wqk_ref[:, db:2 * db], preferred_element_type=F32).astype(BF16)
    dfq_ref[...] = (rope(jnp.dot(h, wqk_ref[:, 2 * db:3 * db], preferred_element_type=F32))
                    * df_scale).astype(BF16)
    dfk_ref[...] = rope(jnp.dot(h, wqk_ref[:, 3 * db:4 * db], preferred_element_type=F32)).astype(BF16)
    sbvt_ref[0, 0] = lax.dot_general(wvt_ref[0:db, :], hperm, _NT,
                                     preferred_element_type=F32).astype(BF16)
    dfvt_ref[0, 0] = lax.dot_general(wvt_ref[db:2 * db, :], h, _NT,
                                     preferred_element_type=F32).astype(BF16)


def _split_in_weights(w_in, db):
    col = lambda i: w_in[:, i * db:(i + 1) * db]
    w_rest = jnp.concatenate([w_in[:, 10 * db:], col(1), col(5), col(9)], axis=1).astype(BF16)
    w_u = col(0).astype(BF16)
    w_qk = jnp.concatenate([col(2), col(3), col(6), col(7)], axis=1).astype(BF16)
    w_vt = jnp.concatenate([col(4).T, col(8).T], axis=0).astype(BF16)
    return w_rest, w_u, w_qk, w_vt


def _in_proj(x2d, norm_w, w_split, cos_t, sin_t, batch, seq):
    t_rows, d = x2d.shape
    db = d // 2
    tm = ATT_TILE
    w_rest, w_u, w_qk, w_vt = w_split
    assert seq % tm == 0 and sum(w.size for w in w_split) == d * N_PROJ_TILES * db
    nk = seq // tm
    n_rest = w_rest.shape[1]
    scale = 1.0 / math.sqrt(HEAD_DIM)
    kern = functools.partial(_in_proj_kernel, sb_scale=scale, df_scale=scale * LOG2E)
    row = lambda i: (i, 0)
    whole = pl.BlockSpec(memory_space=pltpu.VMEM)
    qk_sds = jax.ShapeDtypeStruct((t_rows, db), BF16)
    vt_sds = jax.ShapeDtypeStruct((batch, nk, db, tm), BF16)
    vt_spec = pl.BlockSpec((1, 1, db, tm), lambda i: (i // nk, i % nk, 0, 0))
    nj = db // LANES
    chunks = tm // SSM_CHUNK
    uj_sds = jax.ShapeDtypeStruct((nj, t_rows // SSM_CHUNK, SSM_CHUNK, LANES), F32)
    uj_spec = pl.BlockSpec((nj, chunks, SSM_CHUNK, LANES), lambda i: (0, i, 0, 0))
    return pl.pallas_call(
        kern,
        grid=(t_rows // tm,),
        in_specs=[
            pl.BlockSpec((tm, d), row),
            pl.BlockSpec((1, d), lambda i: (0, 0)),
            whole, whole, whole, whole,
            pl.BlockSpec((tm, db), lambda i: (i % nk, 0)),
            pl.BlockSpec((tm, db), lambda i: (i % nk, 0)),
        ],
        out_specs=[
            pl.BlockSpec((tm, n_rest), row), uj_spec,
            pl.BlockSpec((tm, db), row), pl.BlockSpec((tm, db), row),
            pl.BlockSpec((tm, db), row), pl.BlockSpec((tm, db), row),
            vt_spec, vt_spec,
        ],
        out_shape=[jax.ShapeDtypeStruct((t_rows, n_rest), BF16), uj_sds, qk_sds, qk_sds, qk_sds, qk_sds,
                   vt_sds, vt_sds],
        compiler_params=_cparams(("arbitrary",)),
        name="in_proj",
    )(x2d, norm_w.reshape(1, d), w_rest, w_u, w_qk, w_vt, cos_t, sin_t)


def _rope_tables(seq, width):
    half = HEAD_DIM // 2
    inv_freq = ROPE_THETA ** (-jnp.arange(half, dtype=F32) / half)
    ang = jnp.arange(seq, dtype=jnp.int32).astype(F32)[:, None] * inv_freq[None, :]
    cos, sin = jnp.cos(ang), jnp.sin(ang)
    reps = width // HEAD_DIM
    cos_t = jnp.tile(jnp.concatenate([cos, cos], axis=-1), (1, reps))
    sin_t = jnp.tile(jnp.concatenate([-sin, sin], axis=-1), (1, reps))
    return cos_t, sin_t


SSM_TILE_GROUPS = LANES // SSM_GROUP
SSM_TILE_STATES = SSM_TILE_GROUPS * SSM_STATE


def _expand_groups(a, rows_per_group, cols_per_group, col_groups):
    tg = SSM_TILE_GROUPS
    full = jnp.concatenate([a] * tg, axis=0)
    row = lax.broadcasted_iota(jnp.int32, full.shape, 0)
    col = lax.broadcasted_iota(jnp.int32, full.shape, 1)
    same = row // rows_per_group == (col // cols_per_group) % col_groups
    return jnp.where(same, full, 0.0).astype(BF16)


def _ssm_kernel(u_ref, kc_ref, wre_ref, wim_ref, vre_ref, vim_ref, lam_ref, d_ref, y_ref,
                t_ref, w_ref, v_ref, s_ref, x_ref, *, n_chunks):
    ns = SSM_TILE_STATES
    L = SSM_CHUNK
    tg = SSM_TILE_GROUPS

    @pl.when(pl.program_id(1) == 0)
    def _():
        t_ref[...] = jnp.zeros_like(t_ref)
        for lag in range(L):
            blk = _expand_groups(kc_ref[0, lag], SSM_GROUP, SSM_GROUP, tg)
            for s in range(L - lag):
                t_ref[s * LANES:(s + 1) * LANES, (s + lag) * LANES:(s + lag + 1) * LANES] = blk
        for s in range(L):
            rows = slice(s * LANES, (s + 1) * LANES)
            w_ref[rows, :ns] = _expand_groups(wre_ref[0, L - 1 - s], SSM_GROUP, SSM_STATE, tg)
            w_ref[rows, ns:] = _expand_groups(wim_ref[0, L - 1 - s], SSM_GROUP, SSM_STATE, tg)
        v_ref[:ns, :] = _expand_groups(vre_ref[0], SSM_STATE, SSM_GROUP, tg)
        v_ref[ns:, :] = _expand_groups(vim_ref[0], SSM_STATE, SSM_GROUP, tg)

    u = jnp.concatenate([u_ref[0, :, s, :] for s in range(L)], axis=1)
    ub = u.astype(BF16)
    s_ref[...] = jnp.dot(ub, w_ref[...], preferred_element_type=F32)
    lam_r = lam_ref[0, :, :ns]
    lam_i = lam_ref[0, :, ns:]

    def step(k, carry):
        xr, xi = carry
        row = pl.ds(k, 1)
        x_ref[row, :ns] = xr
        x_ref[row, ns:] = xi
        nxr = lam_r * xr - lam_i * xi + s_ref[row, :ns]
        nxi = lam_r * xi + lam_i * xr + s_ref[row, ns:]
        return nxr, nxi

    zero = jnp.zeros((1, ns), F32)
    lax.fori_loop(0, n_chunks, step, (zero, zero))

    y = jnp.dot(ub, t_ref[...], preferred_element_type=F32) + d_ref[0] * u
    y = y + jnp.dot(x_ref[...].astype(BF16), v_ref[...], preferred_element_type=F32)
    for s in range(L):
        y_ref[0, :, s, :] = y[:, s * LANES:(s + 1) * LANES]


def _ssm_matrices(a_re, a_im, log_dt, b_re, b_im, c_re, c_im, d_skip):
    g, p, c = b_re.shape
    L = SSM_CHUNK
    tg = SSM_TILE_GROUPS
    nj = g // tg
    dt = jnp.exp(log_dt.astype(F32))[:, None]
    mag = jnp.exp(dt * a_re)
    abar_re = mag * jnp.cos(dt * a_im)
    abar_im = mag * jnp.sin(dt * a_im)
    denom = a_re * a_re + a_im * a_im
    coef_re = ((abar_re - 1.0) * a_re + abar_im * a_im) / denom
    coef_im = (abar_im * a_re - (abar_re - 1.0) * a_im) / denom
    bb_re = coef_re[..., None] * b_re - coef_im[..., None] * b_im
    bb_im = coef_re[..., None] * b_im + coef_im[..., None] * b_re
    tau = jnp.arange(L + 1, dtype=F32)[:, None, None]
    pw_mag = jnp.exp(tau * (dt * a_re)[None])
    pw_re = pw_mag * jnp.cos(tau * (dt * a_im)[None])
    pw_im = pw_mag * jnp.sin(tau * (dt * a_im)[None])
    m_re = pw_re[..., None] * bb_re[None] - pw_im[..., None] * bb_im[None]
    m_im = pw_re[..., None] * bb_im[None] + pw_im[..., None] * bb_re[None]
    hi = lax.Precision.HIGHEST
    kern = (jnp.einsum('gcp,tgpd->tgcd', c_re, m_re[:L], precision=hi)
            - jnp.einsum('gcp,tgpd->tgcd', c_im, m_im[:L], precision=hi))
    kc = kern.transpose(0, 3, 1, 2).reshape(L, c, nj, tg * c).transpose(2, 0, 1, 3)

    def state_in(m):
        return m[:L].transpose(0, 3, 1, 2).reshape(L, c, nj, tg * p).transpose(2, 0, 1, 3)

    pr, pi = pw_re[1:], pw_im[1:]
    v_re = (c_re[None] * pr[:, :, None, :] - c_im[None] * pi[:, :, None, :])
    v_im = -(c_re[None] * pi[:, :, None, :] + c_im[None] * pr[:, :, None, :])

    def state_out(v):
        return v.transpose(3, 0, 1, 2).reshape(p, L, nj, tg * c).transpose(2, 0, 1, 3).reshape(
            nj, p, L * LANES)

    lam = jnp.concatenate([pw_re[L].reshape(nj, 1, tg * p), pw_im[L].reshape(nj, 1, tg * p)], axis=-1)
    d_t = jnp.broadcast_to(d_skip.astype(F32).reshape(nj, 1, 1, tg * c), (nj, 1, L, tg * c))
    return (kc, state_in(m_re), state_in(m_im), state_out(v_re), state_out(v_im), lam,
            d_t.reshape(nj, 1, L * LANES))


def _ssm(uj, mats, batch, seq):
    nj, rows, L, lanes = uj.shape
    assert L == SSM_CHUNK and lanes == LANES and seq % L == 0
    nc = seq // L
    width = L * LANES
    ns2 = 2 * SSM_TILE_STATES
    kc, w_re, w_im, v_re, v_im, lam, d_t = mats
    kern = functools.partial(_ssm_kernel, n_chunks=nc)

    def per_tile(a):
        nd = a.ndim - 1
        return pl.BlockSpec((1,) + a.shape[1:], lambda j, b: (j,) + (0,) * nd)

    act = pl.BlockSpec((1, nc, L, LANES), lambda j, b: (j, b, 0, 0))
    params = (kc, w_re, w_im, v_re, v_im, lam, d_t)
    return pl.pallas_call(
        kern,
        grid=(nj, batch),
        in_specs=[act] + [per_tile(a) for a in params],
        out_specs=act,
        out_shape=jax.ShapeDtypeStruct((nj, rows, L, LANES), F32),
        scratch_shapes=[pltpu.VMEM((width, width), BF16), pltpu.VMEM((width, ns2), BF16),
                        pltpu.VMEM((ns2, width), BF16),
                        pltpu.VMEM((nc, ns2), F32), pltpu.VMEM((nc, ns2), F32)],
        compiler_params=_cparams(("arbitrary", "arbitrary")),
        name="ssm_scan",
    )(uj, *params)


def _split_halves(q_ref):
    qf = q_ref[0].astype(F32)
    lane = lax.broadcasted_iota(jnp.int32, qf.shape, 1)
    return [jnp.where((lane // HEAD_DIM) == i, qf, 0.0).astype(BF16) for i in range(2)]


def _att_specs(t, nk):
    q_spec = pl.BlockSpec((1, t, LANES), lambda bi, pi, qi: (bi, qi, pi))
    k_spec = pl.BlockSpec((1, nk, t, LANES), lambda bi, pi, qi: (bi, 0, 0, pi))
    vt_spec = pl.BlockSpec((1, nk, LANES, t), lambda bi, pi, qi: (bi, 0, pi, 0))
    return q_spec, k_spec, vt_spec


def _sb_kernel(q_ref, k_ref, vt_ref, o_ref, acc_ref, carry_ref):
    tk = tq = ATT_TILE
    d = HEAD_DIM
    seg = tk // SUBLANES
    qi = pl.program_id(2)
    qs = _split_halves(q_ref)
    row = lax.broadcasted_iota(jnp.int32, (tk, tq), 0)
    col = lax.broadcasted_iota(jnp.int32, (tk, tq), 1)
    diag_mask = (row % SUBLANES) * seg + row // SUBLANES < col
    rid = lax.broadcasted_iota(jnp.int32, (SUBLANES, tq), 0)

    def scores(kb, hh):
        return lax.dot_general(k_ref[0, kb], qs[hh], _NT, preferred_element_type=F32)

    def weights(z, carry, masked):
        cost = jnp.maximum(z, 0.0) + jnp.log(1.0 + jnp.exp(-jnp.abs(z)))
        if masked:
            cost = jnp.where(diag_mask, cost, 0.0)
        run = jnp.zeros((SUBLANES, tq), F32)
        incl = [None] * seg
        for a in reversed(range(seg)):
            run = run + cost[a * SUBLANES:(a + 1) * SUBLANES]
            incl[a] = run
        offset = jnp.zeros((SUBLANES, tq), F32)
        for r in range(1, SUBLANES):
            offset = offset + jnp.where(rid < r, jnp.broadcast_to(run[r:r + 1], (SUBLANES, tq)), 0.0)
        base = offset if carry is None else offset + carry
        w = jnp.concatenate(
            [jnp.exp((z[a * SUBLANES:(a + 1) * SUBLANES] - incl[a]) - base) for a in range(seg)],
            axis=0)
        if masked:
            w = jnp.where(diag_mask, w, 0.0)
        return w.astype(BF16), jnp.sum(run, axis=0, keepdims=True)

    prev = jnp.maximum(qi - 1, 0)
    no_prev_cost = jnp.where(qi == 0, jnp.inf, 0.0).astype(F32)
    for hh in range(2):
        rows = slice(hh * d, (hh + 1) * d)
        z_diag = scores(qi, hh)
        z_prev = scores(prev, hh)
        w_diag, tot_diag = weights(z_diag, None, True)
        w_prev, tot_prev = weights(z_prev, tot_diag + no_prev_cost, False)
        acc_ref[rows, :] = (jnp.dot(vt_ref[0, qi, rows, :], w_diag, preferred_element_type=F32)
                            + jnp.dot(vt_ref[0, prev, rows, :], w_prev, preferred_element_type=F32))
        carry_ref[hh] = tot_diag + tot_prev

    def not_done():
        return jnp.min(carry_ref[...]) < SB_EXIT_COST

    def cond(state):
        kb, go = state
        return jnp.logical_and(kb >= 0, go)

    def body(state):
        kb, _ = state
        for hh in range(2):
            rows = slice(hh * d, (hh + 1) * d)
            w, tot = weights(scores(kb, hh), carry_ref[hh], False)
            acc_ref[rows, :] += jnp.dot(vt_ref[0, kb, rows, :], w, preferred_element_type=F32)
            carry_ref[hh] += tot
        return kb - 1, not_done()

    lax.while_loop(cond, body, (qi - 2, not_done()))
    o_ref[0] = acc_ref[...].T


def _stick_breaking(q, k, vt, batch, seq):
    t = ATT_TILE
    nk = seq // t
    db = q.shape[-1]
    q_spec, k_spec, vt_spec = _att_specs(t, nk)
    return pl.pallas_call(
        _sb_kernel,
        grid=(batch, db // LANES, nk),
        in_specs=[q_spec, k_spec, vt_spec],
        out_specs=q_spec,
        out_shape=jax.ShapeDtypeStruct((batch, seq, db), F32),
        scratch_shapes=[pltpu.VMEM((LANES, t), F32), pltpu.VMEM((2, 1, t), F32)],
        compiler_params=_cparams(("arbitrary", "arbitrary", "arbitrary")),
        name="stick_breaking",
    )(q.reshape(batch, seq, db), k.reshape(batch, nk, t, db), vt)


def _diff_kernel(q_ref, k_ref, vt_ref, lq1_ref, lk1_ref, lq2_ref, lk2_ref, sw_ref, o_ref,
                 s_ref, p_ref, acc_ref, m_ref, l_ref, *, lam_init):
    tk = ATT_TILE
    tq = q_ref.shape[1]
    nd = tq // tk
    qi = pl.program_id(2)
    qs = _split_halves(q_ref)
    row = lax.broadcasted_iota(jnp.int32, (tk, tq), 0)
    col = lax.broadcasted_iota(jnp.int32, (tk, tq), 1)
    kb0 = qi * nd

    def scores(kb, i):
        return lax.dot_general(k_ref[0, kb], qs[i], _NT, preferred_element_type=F32)

    first_prev = jnp.maximum(kb0 - 1, 0)
    for i in range(2):
        s_d = [jnp.where(row + j * tk <= col, scores(kb0 + j, i), -jnp.inf) for j in range(nd)]
        m = functools.reduce(jnp.maximum, [jnp.max(s, axis=0, keepdims=True) for s in s_d])
        p_d = [jnp.exp2(s - m) for s in s_d]
        m_ref[i] = m
        l_ref[i] = functools.reduce(lambda a, b: a + b, [jnp.sum(p, axis=0, keepdims=True) for p in p_d])
        acc = jnp.zeros(acc_ref.shape[1:], F32)
        for j in range(1, nd):
            acc = acc + jnp.dot(vt_ref[0, kb0 + j], p_d[j].astype(BF16), preferred_element_type=F32)
        acc_ref[i] = acc
        p_ref[i] = p_d[0].astype(BF16)
        s_ref[i] = scores(first_prev, i)

    def body(it, _):
        kb = kb0 - 1 - it
        nxt = jnp.maximum(kb - 1, 0)
        vblk = vt_ref[0, kb + 1]
        for i in range(2):
            pv = jnp.dot(vblk, p_ref[i], preferred_element_type=F32)
            s_next = scores(nxt, i)
            s = s_ref[i]
            m_prev = m_ref[i]
            m_new = jnp.maximum(m_prev, jnp.max(s, axis=0, keepdims=True))
            alpha = jnp.exp2(m_prev - m_new)
            p = jnp.exp2(s - m_new)
            l_ref[i] = alpha * l_ref[i] + jnp.sum(p, axis=0, keepdims=True)
            acc_ref[i] = alpha * (acc_ref[i] + pv)
            m_ref[i] = m_new
            p_ref[i] = p.astype(BF16)
            s_ref[i] = s_next
        return 0

    lax.fori_loop(0, kb0, body, 0)
    vblk = vt_ref[0, 0]
    outs = [(acc_ref[i] + jnp.dot(vblk, p_ref[i], preferred_element_type=F32)) / l_ref[i]
            for i in range(2)]

    lam = (jnp.exp(jnp.sum(lq1_ref[...] * lk1_ref[...], axis=-1, keepdims=True))
           - jnp.exp(jnp.sum(lq2_ref[...] * lk2_ref[...], axis=-1, keepdims=True)) + lam_init)
    o = outs[0] - lam * outs[1]
    ms = jnp.mean(o * o, axis=0, keepdims=True)
    o_ref[0] = (o * lax.rsqrt(ms + NORM_EPS) * sw_ref[...] * (1.0 - lam_init)).T


def _diff_attention(q, k, vt, lq1, lk1, lq2, lk2, subln_w, lam_init, batch, seq):
    t = ATT_TILE
    nk = seq // t
    db = q.shape[-1]
    d = HEAD_DIM
    dv = 2 * d
    assert dv == LANES
    tq = DIFF_Q_TILE
    assert seq % tq == 0 and tq % t == 0
    sw = jnp.broadcast_to(subln_w.astype(F32)[:, None], (dv, tq))
    vec = lambda a: a.astype(F32).reshape(1, d)
    kern = functools.partial(_diff_kernel, lam_init=lam_init)
    small = pl.BlockSpec((1, d), lambda bi, hi, qi: (0, 0))
    _, k_spec, vt_spec = _att_specs(t, nk)
    q_spec = pl.BlockSpec((1, tq, LANES), lambda bi, pi, qi: (bi, qi, pi))
    return pl.pallas_call(
        kern,
        grid=(batch, db // dv, seq // tq),
        in_specs=[q_spec, k_spec, vt_spec, small, small, small, small,
                  pl.BlockSpec((dv, tq), lambda bi, hi, qi: (0, 0))],
        out_specs=q_spec,
        out_shape=jax.ShapeDtypeStruct((batch, seq, db), F32),
        scratch_shapes=[pltpu.VMEM((2, t, tq), F32), pltpu.VMEM((2, t, tq), BF16),
                        pltpu.VMEM((2, dv, tq), F32), pltpu.VMEM((2, 1, tq), F32),
                        pltpu.VMEM((2, 1, tq), F32)],
        compiler_params=_cparams(("arbitrary", "arbitrary", "arbitrary")),
        name="diff_attention",
    )(q.reshape(batch, seq, db), k.reshape(batch, nk, t, db), vt,
      vec(lq1), vec(lk1), vec(lq2), vec(lk2), sw)


def _gelu_tanh(x):
    return 0.5 * x * (1.0 + jnp.tanh(math.sqrt(2.0 / math.pi) * (x + 0.044715 * (x * x * x))))


def _silu(x):
    return x * jax.nn.sigmoid(x)


def _merge_kernel(x_ref, y0_ref, y1_ref, y2_ref, y3_ref, osb_ref, odf_ref, g_ssm_ref, g_sb_ref,
                  g_df_ref, ml0_ref, ml1_ref, ml2_ref, wglu_ref, bglu_ref, bm_ref, wbr_ref, wout_ref,
                  fw_ref, o_ref, *, final_norm):
    d = x_ref.shape[-1]
    tm = x_ref.shape[0]
    y = jnp.concatenate([r[0].reshape(tm, LANES) for r in (y0_ref, y1_ref, y2_ref, y3_ref)], axis=1)
    y = _gelu_tanh(y)
    glu = jnp.dot(y.astype(BF16), wglu_ref[...], preferred_element_type=F32) + bglu_ref[...]
    branches = (y * jax.nn.sigmoid(glu) * _silu(g_ssm_ref[...].astype(F32)),
                osb_ref[...] * _silu(g_sb_ref[...].astype(F32)),
                odf_ref[...] * _silu(g_df_ref[...].astype(F32)))
    logits = (ml0_ref, ml1_ref, ml2_ref)
    merged = None
    for n in range(3):
        gate = jax.nn.sigmoid(logits[n][...].astype(F32) + bm_ref[:, n * d:(n + 1) * d])
        term = gate * jnp.dot(branches[n].astype(BF16), wbr_ref[n], preferred_element_type=F32)
        merged = term if merged is None else merged + term
    out = x_ref[...] + jnp.dot(merged.astype(BF16), wout_ref[...], preferred_element_type=F32)
    if final_norm:
        ms = jnp.mean(out * out, axis=-1, keepdims=True)
        out = out * lax.rsqrt(ms + NORM_EPS) * fw_ref[...]
    o_ref[...] = out


def _merge(x2d, yj, o_sb, o_df, rest, w_glu, b_glu, b_merge, w_branch, w_out, final_w,
           final_norm, *, tm=256):
    t, d = x2d.shape
    db = d // 2
    assert t % tm == 0 and yj.shape[0] == 4
    row = lambda i: (i, 0)
    const2 = lambda i: (0, 0)
    gate_tile0 = (3 * d) // db
    chunks = tm // SSM_CHUNK
    y_spec = lambda j: pl.BlockSpec((1, chunks, SSM_CHUNK, LANES), lambda i: (j, i, 0, 0))
    kern = functools.partial(_merge_kernel, final_norm=final_norm)
    return pl.pallas_call(
        kern,
        grid=(t // tm,),
        in_specs=[
            pl.BlockSpec((tm, d), row),
            y_spec(0), y_spec(1), y_spec(2), y_spec(3),
            pl.BlockSpec((tm, db), row),
            pl.BlockSpec((tm, db), row),
            pl.BlockSpec((tm, db), lambda i: (i, gate_tile0)),
            pl.BlockSpec((tm, db), lambda i: (i, gate_tile0 + 1)),
            pl.BlockSpec((tm, db), lambda i: (i, gate_tile0 + 2)),
            pl.BlockSpec((tm, d), lambda i: (i, 0)),
            pl.BlockSpec((tm, d), lambda i: (i, 1)),
            pl.BlockSpec((tm, d), lambda i: (i, 2)),
            pl.BlockSpec((db, db), const2),
            pl.BlockSpec((1, db), const2),
            pl.BlockSpec((1, 3 * d), const2),
            pl.BlockSpec((3, db, d), lambda i: (0, 0, 0)),
            pl.BlockSpec((d, d), const2),
            pl.BlockSpec((1, d), const2),
        ],
        out_specs=pl.BlockSpec((tm, d), row),
        out_shape=jax.ShapeDtypeStruct((t, d), F32),
        compiler_params=_cparams(("arbitrary",)),
        name="merge_out",
    )(x2d, yj, yj, yj, yj, o_sb, o_df, rest, rest, rest, rest, rest, rest,
      w_glu, b_glu.reshape(1, db).astype(F32), b_merge.reshape(1, 3 * d).astype(F32),
      w_branch, w_out, final_w.reshape(1, d).astype(F32))


def _layer(x, layer_idx, cos_t, sin_t, norm_w, w_in, b_merge, a_re, a_im, log_dt, b_re, b_im,
           c_re, c_im, d_skip, w_glu, b_glu, lq1, lk1, lq2, lk2, subln_w, w_branch, w_out,
           final_w, final_norm):
    w_split = _split_in_weights(w_in, x.shape[-1] // 2)
    mats = _ssm_matrices(a_re, a_im, log_dt, b_re, b_im, c_re, c_im, d_skip)
    return _layer_core(x, layer_idx, cos_t, sin_t, norm_w, w_split, b_merge, mats, w_glu.astype(BF16),
                       b_glu, lq1, lk1, lq2, lk2, subln_w, w_branch.astype(BF16), w_out.astype(BF16),
                       final_w, final_norm)


def _layer_core(x, layer_idx, cos_t, sin_t, norm_w, w_split, b_merge, mats, w_glu, b_glu,
                lq1, lk1, lq2, lk2, subln_w, w_branch, w_out, final_w, final_norm):
    b, s, d = x.shape
    db = d // 2
    x2d = x.reshape(b * s, d)
    rest, uj, sbq, sbk, dfq, dfk, sbvt, dfvt = _in_proj(x2d, norm_w, w_split, cos_t, sin_t, b, s)
    yj = _ssm(uj, mats, b, s)
    o_sb = _stick_breaking(sbq, sbk, sbvt, b, s)
    lam_init = 0.8 - 0.6 * math.exp(-0.3 * layer_idx)
    o_df = _diff_attention(dfq, dfk, dfvt, lq1, lk1, lq2, lk2, subln_w, lam_init, b, s)

    out = _merge(x2d, yj, o_sb.reshape(b * s, db), o_df.reshape(b * s, db),
                 rest, w_glu, b_glu, b_merge, w_branch, w_out, final_w, final_norm)
    return out.reshape(b, s, d)


def kernel(x, norm_w, w_in, b_merge, ssm_a_re, ssm_a_im, ssm_log_dt, ssm_b_re, ssm_b_im, ssm_c_re,
           ssm_c_im, ssm_d, ssm_w_glu, ssm_b_glu, diff_lq1, diff_lk1, diff_lq2, diff_lk2,
           diff_subln_w, w_branch, w_out, final_norm_w):
    depth = norm_w.shape[0]
    seq = x.shape[1]
    db = x.shape[2] // 2
    cos_t, sin_t = _rope_tables(seq, db)
    w_split = jax.vmap(functools.partial(_split_in_weights, db=db))(w_in)
    mats = jax.vmap(_ssm_matrices)(ssm_a_re, ssm_a_im, ssm_log_dt, ssm_b_re, ssm_b_im, ssm_c_re,
                                   ssm_c_im, ssm_d)
    w_glu, w_br, w_o = ssm_w_glu.astype(BF16), w_branch.astype(BF16), w_out.astype(BF16)
    for i in range(depth):
        x = _layer_core(x, i, cos_t, sin_t, norm_w[i], [w[i] for w in w_split], b_merge[i],
                        [m[i] for m in mats], w_glu[i], ssm_b_glu[i], diff_lq1[i], diff_lk1[i],
                        diff_lq2[i], diff_lk2[i], diff_subln_w[i], w_br[i], w_o[i], final_norm_w,
                        i == depth - 1)
    return x
```

```python
import functools
import math

import jax
import jax.numpy as jnp
from jax import lax
from jax.experimental import pallas as pl
from jax.experimental.pallas import tpu as pltpu

F32 = jnp.float32
BF16 = jnp.bfloat16

NORM_EPS = 1e-6
ROPE_THETA = 10000.0
HEAD_DIM = 64
N_PROJ_TILES = 16
(COL_SSM_U, COL_SSM_GATE, COL_SB_Q, COL_SB_K, COL_SB_V, COL_SB_GATE,
 COL_DF_Q, COL_DF_K, COL_DF_V, COL_DF_GATE, COL_LOGITS) = range(11)
SSM_GROUP = 16
SSM_STATE = 64
SSM_CHUNK = 16
SUBLANES = 8
LANES = 128
ATT_TILE = 256
DIFF_Q_TILE = 512
SB_STEP_LANES = 512
DIFF_STEP_LANES = 512
VMEM_LIMIT = 56 * 1024 * 1024
LOG2E = 1.4426950408889634
SB_EXIT_COST = 120.0

_NT = (((1,), (1,)), ((), ()))


def _cparams(sem):
    return pltpu.CompilerParams(dimension_semantics=sem, vmem_limit_bytes=VMEM_LIMIT)


def _in_proj_kernel(x_ref, nw_ref, w_ref, cos_ref, sin_ref,
                    rest_ref, uj_ref, sbq_ref, sbk_ref, dfq_ref, dfk_ref, sbvt_ref, dfvt_ref, *,
                    sb_scale, df_scale):
    t = ATT_TILE
    db = sbq_ref.shape[-1]

    def proj(lhs, tile, n_tiles=1):
        return jnp.dot(lhs, w_ref[:, tile * db:(tile + n_tiles) * db], preferred_element_type=F32)

    x = x_ref[...]
    ms = jnp.mean(x * x, axis=-1, keepdims=True)
    h = (x * lax.rsqrt(ms + NORM_EPS) * nw_ref[...]).astype(BF16)

    r = lax.broadcasted_iota(jnp.int32, (t, t), 0)
    c = lax.broadcasted_iota(jnp.int32, (t, t), 1)
    perm = jnp.where((r % SUBLANES) * (t // SUBLANES) + r // SUBLANES == c, 1.0, 0.0).astype(BF16)
    hperm = jnp.dot(perm, h, preferred_element_type=F32).astype(BF16)

    n_logit = N_PROJ_TILES - COL_LOGITS
    for c0 in range(0, n_logit, 3):
        rest_ref[:, c0 * db:(c0 + 3) * db] = proj(h, COL_LOGITS + c0, 3).astype(rest_ref.dtype)
    for n, tile in enumerate((COL_SSM_GATE, COL_SB_GATE, COL_DF_GATE)):
        rest_ref[:, (n_logit + n) * db:(n_logit + n + 1) * db] = proj(h, tile).astype(rest_ref.dtype)

    u = proj(h, COL_SSM_U)
    for j in range(db // LANES):
        uj_ref[j] = u[:, j * LANES:(j + 1) * LANES].reshape(t // SSM_CHUNK, SSM_CHUNK, LANES)

    def rope(a):
        lane = lax.broadcasted_iota(jnp.int32, a.shape, 1)
        first_half = (lane % HEAD_DIM) < (HEAD_DIM // 2)
        swapped = jnp.where(first_half,
                            pltpu.roll(a, db - HEAD_DIM // 2, 1),
                            pltpu.roll(a, HEAD_DIM // 2, 1))
        return a * cos_ref[...] + swapped * sin_ref[...]

    sbq_ref[...] = (proj(h, COL_SB_Q) * sb_scale).astype(BF16)
    sbk_ref[...] = proj(hperm, COL_SB_K).astype(BF16)
    dfq_ref[...] = (rope(proj(h, COL_DF_Q)) * df_scale).astype(BF16)
    dfk_ref[...] = rope(proj(h, COL_DF_K)).astype(BF16)
    sbvt_ref[0, 0] = proj(hperm, COL_SB_V).astype(BF16).T
    dfvt_ref[0, 0] = proj(h, COL_DF_V).astype(BF16).T


def _in_proj(x2d, norm_w, w_bf, layer, cos_t, sin_t, batch, seq):
    t_rows, d = x2d.shape
    db = d // 2
    tm = ATT_TILE
    assert seq % tm == 0 and w_bf.shape[1:] == (d, N_PROJ_TILES * db)
    nk = seq // tm
    n_rest = (N_PROJ_TILES - COL_LOGITS + 3) * db
    scale = 1.0 / math.sqrt(HEAD_DIM)
    kern = functools.partial(_in_proj_kernel, sb_scale=scale, df_scale=scale * LOG2E)
    row = lambda i: (i, 0)
    layer_weight = pl.BlockSpec((None,) + w_bf.shape[1:], lambda i: (layer, 0, 0),
                                pipeline_mode=pl.Buffered(1))
    qk_sds = jax.ShapeDtypeStruct((t_rows, db), BF16)
    vt_sds = jax.ShapeDtypeStruct((batch, nk, db, tm), BF16)
    vt_spec = pl.BlockSpec((1, 1, db, tm), lambda i: (i // nk, i % nk, 0, 0))
    nj = db // LANES
    chunks = tm // SSM_CHUNK
    uj_sds = jax.ShapeDtypeStruct((nj, t_rows // SSM_CHUNK, SSM_CHUNK, LANES), F32)
    uj_spec = pl.BlockSpec((nj, chunks, SSM_CHUNK, LANES), lambda i: (0, i, 0, 0))
    return pl.pallas_call(
        kern,
        grid=(t_rows // tm,),
        in_specs=[
            pl.BlockSpec((tm, d), row),
            pl.BlockSpec((1, d), lambda i: (0, 0)),
            layer_weight,
            pl.BlockSpec((tm, db), lambda i: (i % nk, 0)),
            pl.BlockSpec((tm, db), lambda i: (i % nk, 0)),
        ],
        out_specs=[
            pl.BlockSpec((tm, n_rest), row), uj_spec,
            pl.BlockSpec((tm, db), row), pl.BlockSpec((tm, db), row),
            pl.BlockSpec((tm, db), row), pl.BlockSpec((tm, db), row),
            vt_spec, vt_spec,
        ],
        out_shape=[jax.ShapeDtypeStruct((t_rows, n_rest), BF16), uj_sds, qk_sds, qk_sds, qk_sds, qk_sds,
                   vt_sds, vt_sds],
        compiler_params=_cparams(("arbitrary",)),
        name="in_proj",
    )(x2d, norm_w.reshape(1, d), w_bf, cos_t, sin_t)


def _rope_tables(seq, width):
    half = HEAD_DIM // 2
    inv_freq = ROPE_THETA ** (-jnp.arange(half, dtype=F32) / half)
    ang = jnp.arange(seq, dtype=jnp.int32).astype(F32)[:, None] * inv_freq[None, :]
    cos, sin = jnp.cos(ang), jnp.sin(ang)
    reps = width // HEAD_DIM
    cos_t = jnp.tile(jnp.concatenate([cos, cos], axis=-1), (1, reps))
    sin_t = jnp.tile(jnp.concatenate([-sin, sin], axis=-1), (1, reps))
    return cos_t, sin_t


SSM_TILE_GROUPS = LANES // SSM_GROUP
SSM_TILE_STATES = SSM_TILE_GROUPS * SSM_STATE


def _expand_groups(a, rows_per_group, cols_per_group, col_groups):
    tg = SSM_TILE_GROUPS
    full = jnp.concatenate([a] * tg, axis=0)
    row = lax.broadcasted_iota(jnp.int32, full.shape, 0)
    col = lax.broadcasted_iota(jnp.int32, full.shape, 1)
    same = row // rows_per_group == (col // cols_per_group) % col_groups
    return jnp.where(same, full, 0.0).astype(BF16)


def _ssm_kernel(u_ref, kc_ref, wre_ref, wim_ref, vre_ref, vim_ref, lam_ref, d_ref, y_ref,
                t_ref, w_ref, v_ref, s_ref, x_ref, *, n_chunks):
    ns = SSM_TILE_STATES
    L = SSM_CHUNK
    tg = SSM_TILE_GROUPS

    @pl.when(pl.program_id(1) == 0)
    def _():
        t_ref[...] = jnp.zeros_like(t_ref)
        for lag in range(L):
            blk = _expand_groups(kc_ref[0, lag], SSM_GROUP, SSM_GROUP, tg)
            for s in range(L - lag):
                t_ref[s * LANES:(s + 1) * LANES, (s + lag) * LANES:(s + lag + 1) * LANES] = blk
        for s in range(L):
            rows = slice(s * LANES, (s + 1) * LANES)
            w_ref[rows, :ns] = _expand_groups(wre_ref[0, L - 1 - s], SSM_GROUP, SSM_STATE, tg)
            w_ref[rows, ns:] = _expand_groups(wim_ref[0, L - 1 - s], SSM_GROUP, SSM_STATE, tg)
        v_ref[:ns, :] = _expand_groups(vre_ref[0], SSM_STATE, SSM_GROUP, tg)
        v_ref[ns:, :] = _expand_groups(vim_ref[0], SSM_STATE, SSM_GROUP, tg)

    u = jnp.concatenate([u_ref[0, :, s, :] for s in range(L)], axis=1)
    ub = u.astype(BF16)
    s_ref[...] = jnp.dot(ub, w_ref[...], preferred_element_type=F32)
    lam_r = lam_ref[0, :, :ns]
    lam_i = lam_ref[0, :, ns:]

    def step(k, carry):
        xr, xi = carry
        row = pl.ds(k, 1)
        x_ref[row, :ns] = xr
        x_ref[row, ns:] = xi
        nxr = lam_r * xr - lam_i * xi + s_ref[row, :ns]
        nxi = lam_r * xi + lam_i * xr + s_ref[row, ns:]
        return nxr, nxi

    zero = jnp.zeros((1, ns), F32)
    lax.fori_loop(0, n_chunks, step, (zero, zero))

    y = jnp.dot(ub, t_ref[...], preferred_element_type=F32) + d_ref[0] * u
    y = y + jnp.dot(x_ref[...].astype(BF16), v_ref[...], preferred_element_type=F32)
    for s in range(L):
        y_ref[0, :, s, :] = y[:, s * LANES:(s + 1) * LANES]


def _ssm_matrices(a_re, a_im, log_dt, b_re, b_im, c_re, c_im, d_skip):
    g, p, c = b_re.shape
    L = SSM_CHUNK
    tg = SSM_TILE_GROUPS
    nj = g // tg
    dt = jnp.exp(log_dt.astype(F32))[:, None]
    mag = jnp.exp(dt * a_re)
    abar_re = mag * jnp.cos(dt * a_im)
    abar_im = mag * jnp.sin(dt * a_im)
    denom = a_re * a_re + a_im * a_im
    coef_re = ((abar_re - 1.0) * a_re + abar_im * a_im) / denom
    coef_im = (abar_im * a_re - (abar_re - 1.0) * a_im) / denom
    bb_re = coef_re[..., None] * b_re - coef_im[..., None] * b_im
    bb_im = coef_re[..., None] * b_im + coef_im[..., None] * b_re
    tau = jnp.arange(L + 1, dtype=F32)[:, None, None]
    pw_mag = jnp.exp(tau * (dt * a_re)[None])
    pw_re = pw_mag * jnp.cos(tau * (dt * a_im)[None])
    pw_im = pw_mag * jnp.sin(tau * (dt * a_im)[None])
    m_re = pw_re[..., None] * bb_re[None] - pw_im[..., None] * bb_im[None]
    m_im = pw_re[..., None] * bb_im[None] + pw_im[..., None] * bb_re[None]
    hi = lax.Precision.HIGHEST
    kern = (jnp.einsum('gcp,tgpd->tgcd', c_re, m_re[:L], precision=hi)
            - jnp.einsum('gcp,tgpd->tgcd', c_im, m_im[:L], precision=hi))
    kc = kern.transpose(0, 3, 1, 2).reshape(L, c, nj, tg * c).transpose(2, 0, 1, 3)

    def state_in(m):
        return m[:L].transpose(0, 3, 1, 2).reshape(L, c, nj, tg * p).transpose(2, 0, 1, 3)

    pr, pi = pw_re[1:], pw_im[1:]
    v_re = (c_re[None] * pr[:, :, None, :] - c_im[None] * pi[:, :, None, :])
    v_im = -(c_re[None] * pi[:, :, None, :] + c_im[None] * pr[:, :, None, :])

    def state_out(v):
        return v.transpose(3, 0, 1, 2).reshape(p, L, nj, tg * c).transpose(2, 0, 1, 3).reshape(
            nj, p, L * LANES)

    lam = jnp.concatenate([pw_re[L].reshape(nj, 1, tg * p), pw_im[L].reshape(nj, 1, tg * p)], axis=-1)
    d_t = jnp.broadcast_to(d_skip.astype(F32).reshape(nj, 1, 1, tg * c), (nj, 1, L, tg * c))
    return (kc, state_in(m_re), state_in(m_im), state_out(v_re), state_out(v_im), lam,
            d_t.reshape(nj, 1, L * LANES))


def _ssm(uj, mats, batch, seq):
    nj, rows, L, lanes = uj.shape
    assert L == SSM_CHUNK and lanes == LANES and seq % L == 0
    nc = seq // L
    width = L * LANES
    ns2 = 2 * SSM_TILE_STATES
    kc, w_re, w_im, v_re, v_im, lam, d_t = mats
    kern = functools.partial(_ssm_kernel, n_chunks=nc)

    def per_tile(a):
        nd = a.ndim - 1
        return pl.BlockSpec((1,) + a.shape[1:], lambda j, b: (j,) + (0,) * nd)

    act = pl.BlockSpec((1, nc, L, LANES), lambda j, b: (j, b, 0, 0))
    params = (kc, w_re, w_im, v_re, v_im, lam, d_t)
    return pl.pallas_call(
        kern,
        grid=(nj, batch),
        in_specs=[act] + [per_tile(a) for a in params],
        out_specs=act,
        out_shape=jax.ShapeDtypeStruct((nj, rows, L, LANES), F32),
        scratch_shapes=[pltpu.VMEM((width, width), BF16), pltpu.VMEM((width, ns2), BF16),
                        pltpu.VMEM((ns2, width), BF16),
                        pltpu.VMEM((nc, ns2), F32), pltpu.VMEM((nc, ns2), F32)],
        compiler_params=_cparams(("arbitrary", "arbitrary")),
        name="ssm_scan",
    )(uj, *params)


def _split_halves(q_ref):
    out = []
    for p in range(q_ref.shape[-1] // LANES):
        qf = q_ref[0, :, p * LANES:(p + 1) * LANES].astype(F32)
        lane = lax.broadcasted_iota(jnp.int32, qf.shape, 1)
        out += [jnp.where((lane // HEAD_DIM) == i, qf, 0.0).astype(BF16) for i in range(2)]
    return out


def _att_specs(tq, t, nk, width=LANES):
    q_spec = pl.BlockSpec((1, tq, width), lambda bi, pi, qi: (bi, qi, pi))
    k_spec = pl.BlockSpec((1, nk, t, width), lambda bi, pi, qi: (bi, 0, 0, pi))
    vt_spec = pl.BlockSpec((1, nk, width, t), lambda bi, pi, qi: (bi, 0, pi, 0))
    return q_spec, k_spec, vt_spec


def _sb_kernel(q_ref, k_ref, vt_ref, o_ref, acc_ref, carry_ref):
    tk = tq = ATT_TILE
    d = HEAD_DIM
    seg = tk // SUBLANES
    qi = pl.program_id(2)
    qs = _split_halves(q_ref)
    n_heads = len(qs)
    row = lax.broadcasted_iota(jnp.int32, (tk, tq), 0)
    col = lax.broadcasted_iota(jnp.int32, (tk, tq), 1)
    diag_mask = (row % SUBLANES) * seg + row // SUBLANES < col
    rid = lax.broadcasted_iota(jnp.int32, (SUBLANES, tq), 0)

    def scores(kb, hh):
        pair = slice((hh // 2) * LANES, (hh // 2 + 1) * LANES)
        return lax.dot_general(k_ref[0, kb, :, pair], qs[hh], _NT, preferred_element_type=F32)

    def weights(z, carry, masked):
        cost = jnp.maximum(z, 0.0) + jnp.log(1.0 + jnp.exp(-jnp.abs(z)))
        if masked:
            cost = jnp.where(diag_mask, cost, 0.0)
        run = jnp.zeros((SUBLANES, tq), F32)
        incl = [None] * seg
        for a in reversed(range(seg)):
            run = run + cost[a * SUBLANES:(a + 1) * SUBLANES]
            incl[a] = run
        offset = jnp.zeros((SUBLANES, tq), F32)
        for r in range(1, SUBLANES):
            offset = offset + jnp.where(rid < r, jnp.broadcast_to(run[r:r + 1], (SUBLANES, tq)), 0.0)
        base = offset if carry is None else offset + carry
        w = jnp.concatenate(
            [jnp.exp((z[a * SUBLANES:(a + 1) * SUBLANES] - incl[a]) - base) for a in range(seg)],
            axis=0)
        if masked:
            w = jnp.where(diag_mask, w, 0.0)
        return w.astype(BF16), jnp.sum(run, axis=0, keepdims=True)

    prev = jnp.maximum(qi - 1, 0)
    no_prev_cost = jnp.where(qi == 0, jnp.inf, 0.0).astype(F32)
    for hh in range(n_heads):
        rows = slice(hh * d, (hh + 1) * d)
        z_diag = scores(qi, hh)
        z_prev = scores(prev, hh)
        w_diag, tot_diag = weights(z_diag, None, True)
        w_prev, tot_prev = weights(z_prev, tot_diag + no_prev_cost, False)
        acc_ref[rows, :] = (jnp.dot(vt_ref[0, qi, rows, :], w_diag, preferred_element_type=F32)
                            + jnp.dot(vt_ref[0, prev, rows, :], w_prev, preferred_element_type=F32))
        carry_ref[hh] = tot_diag + tot_prev

    def not_done():
        return jnp.min(carry_ref[...]) < SB_EXIT_COST

    def cond(state):
        kb, go = state
        return jnp.logical_and(kb >= 0, go)

    def body(state):
        kb, _ = state
        for hh in range(n_heads):
            rows = slice(hh * d, (hh + 1) * d)
            w, tot = weights(scores(kb, hh), carry_ref[hh], False)
            acc_ref[rows, :] += jnp.dot(vt_ref[0, kb, rows, :], w, preferred_element_type=F32)
            carry_ref[hh] += tot
        return kb - 1, not_done()

    lax.while_loop(cond, body, (qi - 2, not_done()))
    o_ref[0] = acc_ref[...].T


def _stick_breaking(q, k, vt, batch, seq):
    t = ATT_TILE
    nk = seq // t
    db = q.shape[-1]
    width = SB_STEP_LANES
    assert db % width == 0
    q_spec, k_spec, vt_spec = _att_specs(t, t, nk, width)
    return pl.pallas_call(
        _sb_kernel,
        grid=(batch, db // width, nk),
        in_specs=[q_spec, k_spec, vt_spec],
        out_specs=q_spec,
        out_shape=jax.ShapeDtypeStruct((batch, seq, db), F32),
        scratch_shapes=[pltpu.VMEM((width, t), F32), pltpu.VMEM((width // HEAD_DIM, 1, t), F32)],
        compiler_params=_cparams(("arbitrary", "arbitrary", "arbitrary")),
        name="stick_breaking",
    )(q.reshape(batch, seq, db), k.reshape(batch, nk, t, db), vt)


def _diff_kernel(q_ref, k_ref, vt_ref, lq1_ref, lk1_ref, lq2_ref, lk2_ref, sw_ref, o_ref,
                 s_ref, p_ref, acc_ref, m_ref, l_ref, *, lam_init):
    tk = ATT_TILE
    tq = q_ref.shape[1]
    nd = tq // tk
    qi = pl.program_id(2)
    qs = _split_halves(q_ref)
    n_streams = len(qs)
    row = lax.broadcasted_iota(jnp.int32, (tk, tq), 0)
    col = lax.broadcasted_iota(jnp.int32, (tk, tq), 1)
    kb0 = qi * nd
    head = lambda i: slice((i // 2) * LANES, (i // 2 + 1) * LANES)

    def scores(kb, i):
        return lax.dot_general(k_ref[0, kb, :, head(i)], qs[i], _NT, preferred_element_type=F32)

    first_prev = jnp.maximum(kb0 - 1, 0)
    for i in range(n_streams):
        s_d = [jnp.where(row + j * tk <= col, scores(kb0 + j, i), -jnp.inf) for j in range(nd)]
        m = functools.reduce(jnp.maximum, [jnp.max(s, axis=0, keepdims=True) for s in s_d])
        p_d = [jnp.exp2(s - m) for s in s_d]
        m_ref[i] = m
        l_ref[i] = functools.reduce(lambda a, b: a + b, [jnp.sum(p, axis=0, keepdims=True) for p in p_d])
        acc = jnp.zeros(acc_ref.shape[1:], F32)
        for j in range(1, nd):
            acc = acc + jnp.dot(vt_ref[0, kb0 + j, head(i), :], p_d[j].astype(BF16),
                                preferred_element_type=F32)
        acc_ref[i] = acc
        p_ref[i] = p_d[0].astype(BF16)
        s_ref[i] = scores(first_prev, i)

    def body(it, _):
        kb = kb0 - 1 - it
        nxt = jnp.maximum(kb - 1, 0)
        for i in range(n_streams):
            pv = jnp.dot(vt_ref[0, kb + 1, head(i), :], p_ref[i], preferred_element_type=F32)
            s_next = scores(nxt, i)
            s = s_ref[i]
            m_prev = m_ref[i]
            m_new = jnp.maximum(m_prev, jnp.max(s, axis=0, keepdims=True))
            alpha = jnp.exp2(m_prev - m_new)
            p = jnp.exp2(s - m_new)
            l_ref[i] = alpha * l_ref[i] + jnp.sum(p, axis=0, keepdims=True)
            acc_ref[i] = alpha * (acc_ref[i] + pv)
            m_ref[i] = m_new
            p_ref[i] = p.astype(BF16)
            s_ref[i] = s_next
        return 0

    lax.fori_loop(0, kb0, body, 0)
    outs = [(acc_ref[i] + jnp.dot(vt_ref[0, 0, head(i), :], p_ref[i], preferred_element_type=F32))
            / l_ref[i] for i in range(n_streams)]

    lam = (jnp.exp(jnp.sum(lq1_ref[...] * lk1_ref[...], axis=-1, keepdims=True))
           - jnp.exp(jnp.sum(lq2_ref[...] * lk2_ref[...], axis=-1, keepdims=True)) + lam_init)
    for h in range(n_streams // 2):
        o = outs[2 * h] - lam * outs[2 * h + 1]
        ms = jnp.mean(o * o, axis=0, keepdims=True)
        o_ref[0, :, h * LANES:(h + 1) * LANES] = (
            o * lax.rsqrt(ms + NORM_EPS) * sw_ref[...] * (1.0 - lam_init)).T


def _diff_attention(q, k, vt, lq1, lk1, lq2, lk2, subln_w, lam_init, batch, seq):
    t = ATT_TILE
    nk = seq // t
    db = q.shape[-1]
    d = HEAD_DIM
    dv = 2 * d
    assert dv == LANES
    tq = DIFF_Q_TILE
    assert seq % tq == 0 and tq % t == 0
    sw = jnp.broadcast_to(subln_w.astype(F32)[:, None], (dv, tq))
    vec = lambda a: a.astype(F32).reshape(1, d)
    kern = functools.partial(_diff_kernel, lam_init=lam_init)
    small = pl.BlockSpec((1, d), lambda bi, hi, qi: (0, 0))
    width = DIFF_STEP_LANES
    ns = 2 * (width // dv)
    assert db % width == 0
    q_spec, k_spec, vt_spec = _att_specs(tq, t, nk, width)
    return pl.pallas_call(
        kern,
        grid=(batch, db // width, seq // tq),
        in_specs=[q_spec, k_spec, vt_spec, small, small, small, small,
                  pl.BlockSpec((dv, tq), lambda bi, hi, qi: (0, 0))],
        out_specs=q_spec,
        out_shape=jax.ShapeDtypeStruct((batch, seq, db), F32),
        scratch_shapes=[pltpu.VMEM((ns, t, tq), F32), pltpu.VMEM((ns, t, tq), BF16),
                        pltpu.VMEM((ns, dv, tq), F32), pltpu.VMEM((ns, 1, tq), F32),
                        pltpu.VMEM((ns, 1, tq), F32)],
        compiler_params=_cparams(("arbitrary", "arbitrary", "arbitrary")),
        name="diff_attention",
    )(q.reshape(batch, seq, db), k.reshape(batch, nk, t, db), vt,
      vec(lq1), vec(lk1), vec(lq2), vec(lk2), sw)


def _gelu_tanh(x):
    return 0.5 * x * (1.0 + jnp.tanh(math.sqrt(2.0 / math.pi) * (x + 0.044715 * (x * x * x))))


def _silu(x):
    return x * jax.nn.sigmoid(x)


def _merge_kernel(x_ref, y0_ref, y1_ref, y2_ref, y3_ref, osb_ref, odf_ref, g_ssm_ref, g_sb_ref,
                  g_df_ref, ml0_ref, ml1_ref, ml2_ref, wglu_ref, bglu_ref, bm_ref, wbr_ref, wout_ref,
                  fw_ref, o_ref, *, final_norm):
    d = x_ref.shape[-1]
    tm = x_ref.shape[0]
    y = jnp.concatenate([r[0].reshape(tm, LANES) for r in (y0_ref, y1_ref, y2_ref, y3_ref)], axis=1)
    y = _gelu_tanh(y)
    glu = jnp.dot(y.astype(BF16), wglu_ref[...], preferred_element_type=F32) + bglu_ref[...]
    branches = (y * jax.nn.sigmoid(glu) * _silu(g_ssm_ref[...].astype(F32)),
                osb_ref[...] * _silu(g_sb_ref[...].astype(F32)),
                odf_ref[...] * _silu(g_df_ref[...].astype(F32)))
    logits = (ml0_ref, ml1_ref, ml2_ref)
    merged = None
    for n in range(3):
        gate = jax.nn.sigmoid(logits[n][...].astype(F32) + bm_ref[:, n * d:(n + 1) * d])
        term = gate * jnp.dot(branches[n].astype(BF16), wbr_ref[n], preferred_element_type=F32)
        merged = term if merged is None else merged + term
    out = x_ref[...] + jnp.dot(merged.astype(BF16), wout_ref[...], preferred_element_type=F32)
    if final_norm:
        ms = jnp.mean(out * out, axis=-1, keepdims=True)
        out = out * lax.rsqrt(ms + NORM_EPS) * fw_ref[...]
    o_ref[...] = out


def _merge(x2d, yj, o_sb, o_df, rest, w_glu, b_glu, b_merge, w_branch, w_out, final_w,
           final_norm, *, tm=256):
    t, d = x2d.shape
    db = d // 2
    assert t % tm == 0 and yj.shape[0] == 4
    row = lambda i: (i, 0)
    const2 = lambda i: (0, 0)
    gate_tile0 = (3 * d) // db
    chunks = tm // SSM_CHUNK
    y_spec = lambda j: pl.BlockSpec((1, chunks, SSM_CHUNK, LANES), lambda i: (j, i, 0, 0))
    kern = functools.partial(_merge_kernel, final_norm=final_norm)
    return pl.pallas_call(
        kern,
        grid=(t // tm,),
        in_specs=[
            pl.BlockSpec((tm, d), row),
            y_spec(0), y_spec(1), y_spec(2), y_spec(3),
            pl.BlockSpec((tm, db), row),
            pl.BlockSpec((tm, db), row),
            pl.BlockSpec((tm, db), lambda i: (i, gate_tile0)),
            pl.BlockSpec((tm, db), lambda i: (i, gate_tile0 + 1)),
            pl.BlockSpec((tm, db), lambda i: (i, gate_tile0 + 2)),
            pl.BlockSpec((tm, d), lambda i: (i, 0)),
            pl.BlockSpec((tm, d), lambda i: (i, 1)),
            pl.BlockSpec((tm, d), lambda i: (i, 2)),
            pl.BlockSpec((db, db), const2),
            pl.BlockSpec((1, db), const2),
            pl.BlockSpec((1, 3 * d), const2),
            pl.BlockSpec((3, db, d), lambda i: (0, 0, 0)),
            pl.BlockSpec((d, d), const2),
            pl.BlockSpec((1, d), const2),
        ],
        out_specs=pl.BlockSpec((tm, d), row),
        out_shape=jax.ShapeDtypeStruct((t, d), F32),
        compiler_params=_cparams(("arbitrary",)),
        name="merge_out",
    )(x2d, yj, yj, yj, yj, o_sb, o_df, rest, rest, rest, rest, rest, rest,
      w_glu, b_glu.reshape(1, db).astype(F32), b_merge.reshape(1, 3 * d).astype(F32),
      w_branch, w_out, final_w.reshape(1, d).astype(F32))


def _layer(x, layer_idx, cos_t, sin_t, norm_w, w_in, b_merge, a_re, a_im, log_dt, b_re, b_im,
           c_re, c_im, d_skip, w_glu, b_glu, lq1, lk1, lq2, lk2, subln_w, w_branch, w_out,
           final_w, final_norm):
    mats = _ssm_matrices(a_re, a_im, log_dt, b_re, b_im, c_re, c_im, d_skip)
    return _layer_core(x, layer_idx, 0, cos_t, sin_t, norm_w, w_in[None].astype(BF16), b_merge, mats,
                       w_glu.astype(BF16), b_glu, lq1, lk1, lq2, lk2, subln_w, w_branch.astype(BF16),
                       w_out.astype(BF16), final_w, final_norm)


def _layer_core(x, layer_idx, w_slot, cos_t, sin_t, norm_w, w_bf, b_merge, mats, w_glu, b_glu,
                lq1, lk1, lq2, lk2, subln_w, w_branch, w_out, final_w, final_norm):
    b, s, d = x.shape
    db = d // 2
    x2d = x.reshape(b * s, d)
    rest, uj, sbq, sbk, dfq, dfk, sbvt, dfvt = _in_proj(x2d, norm_w, w_bf, w_slot, cos_t, sin_t, b, s)
    yj = _ssm(uj, mats, b, s)
    o_sb = _stick_breaking(sbq, sbk, sbvt, b, s)
    lam_init = 0.8 - 0.6 * math.exp(-0.3 * layer_idx)
    o_df = _diff_attention(dfq, dfk, dfvt, lq1, lk1, lq2, lk2, subln_w, lam_init, b, s)

    out = _merge(x2d, yj, o_sb.reshape(b * s, db), o_df.reshape(b * s, db),
                 rest, w_glu, b_glu, b_merge, w_branch, w_out, final_w, final_norm)
    return out.reshape(b, s, d)


def kernel(x, norm_w, w_in, b_merge, ssm_a_re, ssm_a_im, ssm_log_dt, ssm_b_re, ssm_b_im, ssm_c_re,
           ssm_c_im, ssm_d, ssm_w_glu, ssm_b_glu, diff_lq1, diff_lk1, diff_lq2, diff_lk2,
           diff_subln_w, w_branch, w_out, final_norm_w):
    depth = norm_w.shape[0]
    seq = x.shape[1]
    db = x.shape[2] // 2
    cos_t, sin_t = _rope_tables(seq, db)
    w_bf = w_in.astype(BF16)
    mats = jax.vmap(_ssm_matrices)(ssm_a_re, ssm_a_im, ssm_log_dt, ssm_b_re, ssm_b_im, ssm_c_re,
                                   ssm_c_im, ssm_d)
    w_glu, w_br, w_o = ssm_w_glu.astype(BF16), w_branch.astype(BF16), w_out.astype(BF16)
    for i in range(depth):
        x = _layer_core(x, i, i, cos_t, sin_t, norm_w[i], w_bf, b_merge[i],
                        [m[i] for m in mats], w_glu[i], ssm_b_glu[i], diff_lq1[i], diff_lk1[i],
                        diff_lq2[i], diff_lk2[i], diff_subln_w[i], w_br[i], w_o[i], final_norm_w,
                        i == depth - 1)
    return x
```

```python
import functools
import math

import jax
import jax.numpy as jnp
from jax import lax
from jax.experimental import pallas as pl
from jax.experimental.pallas import tpu as pltpu

F32 = jnp.float32
BF16 = jnp.bfloat16

NORM_EPS = 1e-6
ROPE_THETA = 10000.0
HEAD_DIM = 64
N_PROJ_TILES = 16
(COL_SSM_U, COL_SSM_GATE, COL_SB_Q, COL_SB_K, COL_SB_V, COL_SB_GATE,
 COL_DF_Q, COL_DF_K, COL_DF_V, COL_DF_GATE, COL_LOGITS) = range(11)
SSM_GROUP = 16
SSM_STATE = 64
SSM_CHUNK = 16
SUBLANES = 8
BF16_SUBLANES = 16
LANES = 128
ATT_TILE = 256
DIFF_Q_TILE = 512
SB_STEP_LANES = 512
DIFF_STEP_LANES = 512
VMEM_LIMIT = 56 * 1024 * 1024
LOG2E = 1.4426950408889634
SB_EXIT_COST = 120.0

_NT = (((1,), (1,)), ((), ()))


def _cparams(sem):
    return pltpu.CompilerParams(dimension_semantics=sem, vmem_limit_bytes=VMEM_LIMIT)


def _in_proj_kernel(x_ref, nw_ref, w_ref, cos_ref, sin_ref,
                    rest_ref, uj_ref, sbq_ref, sbk_ref, dfq_ref, dfk_ref, sbvt_ref, dfvt_ref, *,
                    sb_scale, df_scale):
    t = ATT_TILE
    db = sbq_ref.shape[-1]

    def proj(lhs, tile, n_tiles=1):
        return jnp.dot(lhs, w_ref[:, tile * db:(tile + n_tiles) * db], preferred_element_type=F32)

    x = x_ref[...]
    ms = jnp.mean(x * x, axis=-1, keepdims=True)
    h = (x * lax.rsqrt(ms + NORM_EPS) * nw_ref[...]).astype(BF16)

    r = lax.broadcasted_iota(jnp.int32, (t, t), 0)
    c = lax.broadcasted_iota(jnp.int32, (t, t), 1)
    perm = jnp.where((r % SUBLANES) * (t // SUBLANES) + r // SUBLANES == c, 1.0, 0.0).astype(BF16)
    hperm = jnp.dot(perm, h, preferred_element_type=F32).astype(BF16)

    n_logit = N_PROJ_TILES - COL_LOGITS
    for c0 in range(0, n_logit, 3):
        rest_ref[:, c0 * db:(c0 + 3) * db] = proj(h, COL_LOGITS + c0, 3).astype(rest_ref.dtype)
    for n, tile in enumerate((COL_SSM_GATE, COL_SB_GATE, COL_DF_GATE)):
        rest_ref[:, (n_logit + n) * db:(n_logit + n + 1) * db] = proj(h, tile).astype(rest_ref.dtype)

    u = proj(h, COL_SSM_U)
    for j in range(db // LANES):
        uj_ref[j] = u[:, j * LANES:(j + 1) * LANES]

    def rope(a):
        lane = lax.broadcasted_iota(jnp.int32, a.shape, 1)
        first_half = (lane % HEAD_DIM) < (HEAD_DIM // 2)
        swapped = jnp.where(first_half,
                            pltpu.roll(a, db - HEAD_DIM // 2, 1),
                            pltpu.roll(a, HEAD_DIM // 2, 1))
        return a * cos_ref[...] + swapped * sin_ref[...]

    sbq_ref[...] = (proj(h, COL_SB_Q) * sb_scale).astype(BF16)
    sbk_ref[...] = proj(hperm, COL_SB_K).astype(BF16)
    dfq_ref[...] = (rope(proj(h, COL_DF_Q)) * df_scale).astype(BF16)
    dfk_ref[...] = rope(proj(h, COL_DF_K)).astype(BF16)
    sbvt_ref[0, 0] = proj(hperm, COL_SB_V).astype(BF16).T
    dfvt_ref[0, 0] = proj(h, COL_DF_V).astype(BF16).T


def _in_proj(x2d, norm_w, w_bf, layer, cos_t, sin_t, batch, seq):
    t_rows, d = x2d.shape
    db = d // 2
    tm = ATT_TILE
    assert seq % tm == 0 and w_bf.shape[1:] == (d, N_PROJ_TILES * db)
    nk = seq // tm
    n_rest = (N_PROJ_TILES - COL_LOGITS + 3) * db
    scale = 1.0 / math.sqrt(HEAD_DIM)
    kern = functools.partial(_in_proj_kernel, sb_scale=scale, df_scale=scale * LOG2E)
    row = lambda i: (i, 0)
    layer_weight = pl.BlockSpec((None,) + w_bf.shape[1:], lambda i: (layer, 0, 0),
                                pipeline_mode=pl.Buffered(1))
    qk_sds = jax.ShapeDtypeStruct((t_rows, db), BF16)
    vt_sds = jax.ShapeDtypeStruct((batch, nk, db, tm), BF16)
    vt_spec = pl.BlockSpec((1, 1, db, tm), lambda i: (i // nk, i % nk, 0, 0))
    nj = db // LANES
    uj_sds = jax.ShapeDtypeStruct((nj, t_rows, LANES), F32)
    uj_spec = pl.BlockSpec((nj, tm, LANES), lambda i: (0, i, 0))
    return pl.pallas_call(
        kern,
        grid=(t_rows // tm,),
        in_specs=[
            pl.BlockSpec((tm, d), row),
            pl.BlockSpec((1, d), lambda i: (0, 0)),
            layer_weight,
            pl.BlockSpec((tm, db), lambda i: (i % nk, 0)),
            pl.BlockSpec((tm, db), lambda i: (i % nk, 0)),
        ],
        out_specs=[
            pl.BlockSpec((tm, n_rest), row), uj_spec,
            pl.BlockSpec((tm, db), row), pl.BlockSpec((tm, db), row),
            pl.BlockSpec((tm, db), row), pl.BlockSpec((tm, db), row),
            vt_spec, vt_spec,
        ],
        out_shape=[jax.ShapeDtypeStruct((t_rows, n_rest), BF16), uj_sds, qk_sds, qk_sds, qk_sds, qk_sds,
                   vt_sds, vt_sds],
        compiler_params=_cparams(("arbitrary",)),
        name="in_proj",
    )(x2d, norm_w.reshape(1, d), w_bf, cos_t, sin_t)


def _rope_tables(seq, width):
    half = HEAD_DIM // 2
    inv_freq = ROPE_THETA ** (-jnp.arange(half, dtype=F32) / half)
    ang = jnp.arange(seq, dtype=jnp.int32).astype(F32)[:, None] * inv_freq[None, :]
    cos, sin = jnp.cos(ang), jnp.sin(ang)
    reps = width // HEAD_DIM
    cos_t = jnp.tile(jnp.concatenate([cos, cos], axis=-1), (1, reps))
    sin_t = jnp.tile(jnp.concatenate([-sin, sin], axis=-1), (1, reps))
    return cos_t, sin_t


SSM_TILE_GROUPS = LANES // SSM_GROUP
SSM_TILE_STATES = SSM_TILE_GROUPS * SSM_STATE


def _expand_groups(a, rows_per_group, cols_per_group, col_groups):
    tg = SSM_TILE_GROUPS
    full = jnp.concatenate([a] * tg, axis=0)
    row = lax.broadcasted_iota(jnp.int32, full.shape, 0)
    col = lax.broadcasted_iota(jnp.int32, full.shape, 1)
    same = row // rows_per_group == (col // cols_per_group) % col_groups
    return jnp.where(same, full, 0.0).astype(BF16)


def _ssm_kernel(u_ref, kc_ref, wre_ref, wim_ref, vre_ref, vim_ref, lam_ref, d_ref, y_ref,
                t_ref, w_ref, v_ref, s_ref, x_ref, *, n_chunks):
    ns = SSM_TILE_STATES
    L = SSM_CHUNK
    tg = SSM_TILE_GROUPS

    @pl.when(pl.program_id(1) == 0)
    def _():
        t_ref[...] = jnp.zeros_like(t_ref)
        for lag in range(L):
            blk = _expand_groups(kc_ref[0, lag], SSM_GROUP, SSM_GROUP, tg)
            for s in range(L - lag):
                t_ref[s * LANES:(s + 1) * LANES, (s + lag) * LANES:(s + lag + 1) * LANES] = blk
        for s in range(L):
            rows = slice(s * LANES, (s + 1) * LANES)
            w_ref[rows, :ns] = _expand_groups(wre_ref[0, L - 1 - s], SSM_GROUP, SSM_STATE, tg)
            w_ref[rows, ns:] = _expand_groups(wim_ref[0, L - 1 - s], SSM_GROUP, SSM_STATE, tg)
        v_ref[:ns, :] = _expand_groups(vre_ref[0], SSM_STATE, SSM_GROUP, tg)
        v_ref[ns:, :] = _expand_groups(vim_ref[0], SSM_STATE, SSM_GROUP, tg)

    u = jnp.concatenate([u_ref[0, pl.ds(s, n_chunks, stride=L), :] for s in range(L)], axis=1)
    ub = u.astype(BF16)
    s_ref[...] = jnp.dot(ub, w_ref[...], preferred_element_type=F32)
    lam_r = lam_ref[0, :, :ns]
    lam_i = lam_ref[0, :, ns:]

    def step(k, carry):
        xr, xi = carry
        row = pl.ds(k, 1)
        x_ref[row, :ns] = xr
        x_ref[row, ns:] = xi
        nxr = lam_r * xr - lam_i * xi + s_ref[row, :ns]
        nxi = lam_r * xi + lam_i * xr + s_ref[row, ns:]
        return nxr, nxi

    zero = jnp.zeros((1, ns), F32)
    lax.fori_loop(0, n_chunks, step, (zero, zero))

    cols = 2 * LANES
    y = jnp.concatenate(
        [jnp.dot(ub[:, :c + cols], t_ref[:c + cols, c:c + cols], preferred_element_type=F32)
         for c in range(0, L * LANES, cols)], axis=1) + d_ref[0] * u
    y = y + jnp.dot(x_ref[...].astype(BF16), v_ref[...], preferred_element_type=F32)
    for s in range(L):
        y_ref[0, pl.ds(s, n_chunks, stride=L), :] = y[:, s * LANES:(s + 1) * LANES]


def _ssm_matrices(a_re, a_im, log_dt, b_re, b_im, c_re, c_im, d_skip):
    g, p, c = b_re.shape
    L = SSM_CHUNK
    tg = SSM_TILE_GROUPS
    nj = g // tg
    dt = jnp.exp(log_dt.astype(F32))[:, None]
    mag = jnp.exp(dt * a_re)
    abar_re = mag * jnp.cos(dt * a_im)
    abar_im = mag * jnp.sin(dt * a_im)
    denom = a_re * a_re + a_im * a_im
    coef_re = ((abar_re - 1.0) * a_re + abar_im * a_im) / denom
    coef_im = (abar_im * a_re - (abar_re - 1.0) * a_im) / denom
    bb_re = coef_re[..., None] * b_re - coef_im[..., None] * b_im
    bb_im = coef_re[..., None] * b_im + coef_im[..., None] * b_re
    tau = jnp.arange(L + 1, dtype=F32)[:, None, None]
    pw_mag = jnp.exp(tau * (dt * a_re)[None])
    pw_re = pw_mag * jnp.cos(tau * (dt * a_im)[None])
    pw_im = pw_mag * jnp.sin(tau * (dt * a_im)[None])
    m_re = pw_re[..., None] * bb_re[None] - pw_im[..., None] * bb_im[None]
    m_im = pw_re[..., None] * bb_im[None] + pw_im[..., None] * bb_re[None]
    hi = lax.Precision.HIGHEST
    kern = (jnp.einsum('gcp,tgpd->tgcd', c_re, m_re[:L], precision=hi)
            - jnp.einsum('gcp,tgpd->tgcd', c_im, m_im[:L], precision=hi))
    kc = kern.transpose(0, 3, 1, 2).reshape(L, c, nj, tg * c).transpose(2, 0, 1, 3)

    def state_in(m):
        return m[:L].transpose(0, 3, 1, 2).reshape(L, c, nj, tg * p).transpose(2, 0, 1, 3)

    pr, pi = pw_re[1:], pw_im[1:]
    v_re = (c_re[None] * pr[:, :, None, :] - c_im[None] * pi[:, :, None, :])
    v_im = -(c_re[None] * pi[:, :, None, :] + c_im[None] * pr[:, :, None, :])

    def state_out(v):
        return v.transpose(3, 0, 1, 2).reshape(p, L, nj, tg * c).transpose(2, 0, 1, 3).reshape(
            nj, p, L * LANES)

    lam = jnp.concatenate([pw_re[L].reshape(nj, 1, tg * p), pw_im[L].reshape(nj, 1, tg * p)], axis=-1)
    d_t = jnp.broadcast_to(d_skip.astype(F32).reshape(nj, 1, 1, tg * c), (nj, 1, L, tg * c))
    return (kc, state_in(m_re), state_in(m_im), state_out(v_re), state_out(v_im), lam,
            d_t.reshape(nj, 1, L * LANES))


def _ssm(uj, mats, batch, seq):
    nj, rows, lanes = uj.shape
    L = SSM_CHUNK
    assert lanes == LANES and seq % L == 0 and rows == batch * seq
    nc = seq // L
    width = L * LANES
    ns2 = 2 * SSM_TILE_STATES
    kc, w_re, w_im, v_re, v_im, lam, d_t = mats
    kern = functools.partial(_ssm_kernel, n_chunks=nc)

    def per_tile(a):
        nd = a.ndim - 1
        return pl.BlockSpec((1,) + a.shape[1:], lambda j, b: (j,) + (0,) * nd)

    act = pl.BlockSpec((1, seq, LANES), lambda j, b: (j, b, 0))
    params = (kc, w_re, w_im, v_re, v_im, lam, d_t)
    return pl.pallas_call(
        kern,
        grid=(nj, batch),
        in_specs=[act] + [per_tile(a) for a in params],
        out_specs=act,
        out_shape=jax.ShapeDtypeStruct((nj, rows, LANES), F32),
        scratch_shapes=[pltpu.VMEM((width, width), BF16), pltpu.VMEM((width, ns2), BF16),
                        pltpu.VMEM((ns2, width), BF16),
                        pltpu.VMEM((nc, ns2), F32), pltpu.VMEM((nc, ns2), F32)],
        compiler_params=_cparams(("arbitrary", "arbitrary")),
        name="ssm_scan",
    )(uj, *params)


def _split_halves(q_ref):
    out = []
    for p in range(q_ref.shape[-1] // LANES):
        qf = q_ref[0, :, p * LANES:(p + 1) * LANES].astype(F32)
        lane = lax.broadcasted_iota(jnp.int32, qf.shape, 1)
        out += [jnp.where((lane // HEAD_DIM) == i, qf, 0.0).astype(BF16) for i in range(2)]
    return out


def _att_specs(tq, t, nk, width=LANES):
    q_spec = pl.BlockSpec((1, tq, width), lambda bi, pi, qi: (bi, qi, pi))
    k_spec = pl.BlockSpec((1, nk, t, width), lambda bi, pi, qi: (bi, 0, 0, pi))
    vt_spec = pl.BlockSpec((1, nk, width, t), lambda bi, pi, qi: (bi, 0, pi, 0))
    return q_spec, k_spec, vt_spec


def _sb_kernel(q_ref, k_ref, vt_ref, o_ref, acc_ref, carry_ref):
    tk = tq = ATT_TILE
    d = HEAD_DIM
    seg = tk // SUBLANES
    qi = pl.program_id(2)
    qs = _split_halves(q_ref)
    n_heads = len(qs)
    row = lax.broadcasted_iota(jnp.int32, (tk, tq), 0)
    col = lax.broadcasted_iota(jnp.int32, (tk, tq), 1)
    diag_mask = (row % SUBLANES) * seg + row // SUBLANES < col
    rid = lax.broadcasted_iota(jnp.int32, (SUBLANES, tq), 0)

    def scores(kb, hh):
        pair = slice((hh // 2) * LANES, (hh // 2 + 1) * LANES)
        return lax.dot_general(k_ref[0, kb, :, pair], qs[hh], _NT, preferred_element_type=F32)

    def weights(z, carry, masked):
        cost = jnp.maximum(z, 0.0) + jnp.log(1.0 + jnp.exp(-jnp.abs(z)))
        if masked:
            cost = jnp.where(diag_mask, cost, 0.0)
        run = jnp.zeros((SUBLANES, tq), F32)
        incl = [None] * seg
        for a in reversed(range(seg)):
            run = run + cost[a * SUBLANES:(a + 1) * SUBLANES]
            incl[a] = run
        offset = jnp.zeros((SUBLANES, tq), F32)
        for r in range(1, SUBLANES):
            offset = offset + jnp.where(rid < r, jnp.broadcast_to(run[r:r + 1], (SUBLANES, tq)), 0.0)
        base = offset if carry is None else offset + carry
        w = jnp.concatenate(
            [jnp.exp((z[a * SUBLANES:(a + 1) * SUBLANES] - incl[a]) - base) for a in range(seg)],
            axis=0)
        if masked:
            w = jnp.where(diag_mask, w, 0.0)
        return w.astype(BF16), jnp.sum(run, axis=0, keepdims=True)

    prev = jnp.maximum(qi - 1, 0)
    no_prev_cost = jnp.where(qi == 0, jnp.inf, 0.0).astype(F32)
    for hh in range(n_heads):
        rows = slice(hh * d, (hh + 1) * d)
        z_diag = scores(qi, hh)
        z_prev = scores(prev, hh)
        w_diag, tot_diag = weights(z_diag, None, True)
        w_prev, tot_prev = weights(z_prev, tot_diag + no_prev_cost, False)
        acc_ref[rows, :] = (jnp.dot(vt_ref[0, qi, rows, :], w_diag, preferred_element_type=F32)
                            + jnp.dot(vt_ref[0, prev, rows, :], w_prev, preferred_element_type=F32))
        carry_ref[hh] = tot_diag + tot_prev

    def not_done():
        return jnp.min(carry_ref[...]) < SB_EXIT_COST

    def cond(state):
        kb, go = state
        return jnp.logical_and(kb >= 0, go)

    def body(state):
        kb, _ = state
        for hh in range(n_heads):
            rows = slice(hh * d, (hh + 1) * d)
            w, tot = weights(scores(kb, hh), carry_ref[hh], False)
            acc_ref[rows, :] += jnp.dot(vt_ref[0, kb, rows, :], w, preferred_element_type=F32)
            carry_ref[hh] += tot
        return kb - 1, not_done()

    lax.while_loop(cond, body, (qi - 2, not_done()))
    o_ref[0] = acc_ref[...].T


def _stick_breaking(q, k, vt, batch, seq):
    t = ATT_TILE
    nk = seq // t
    db = q.shape[-1]
    width = SB_STEP_LANES
    assert db % width == 0
    q_spec, k_spec, vt_spec = _att_specs(t, t, nk, width)
    return pl.pallas_call(
        _sb_kernel,
        grid=(batch, db // width, nk),
        in_specs=[q_spec, k_spec, vt_spec],
        out_specs=q_spec,
        out_shape=jax.ShapeDtypeStruct((batch, seq, db), F32),
        scratch_shapes=[pltpu.VMEM((width, t), F32), pltpu.VMEM((width // HEAD_DIM, 1, t), F32)],
        compiler_params=_cparams(("arbitrary", "arbitrary", "arbitrary")),
        name="stick_breaking",
    )(q.reshape(batch, seq, db), k.reshape(batch, nk, t, db), vt)


def _diff_kernel(q_ref, k_ref, vt_ref, lq1_ref, lk1_ref, lq2_ref, lk2_ref, sw_ref, o_ref,
                 s_ref, p_ref, acc_ref, m_ref, *, lam_init):
    tk = ATT_TILE
    tq = q_ref.shape[1]
    nd = tq // tk
    qi = pl.program_id(2)
    qs = _split_halves(q_ref)
    n_streams = len(qs)
    row = lax.broadcasted_iota(jnp.int32, (tk, tq), 0)
    col = lax.broadcasted_iota(jnp.int32, (tk, tq), 1)
    kb0 = qi * nd
    head = lambda i: slice((i // 2) * LANES, (i // 2 + 1) * LANES)

    def scores(kb, i):
        return lax.dot_general(k_ref[0, kb, :, head(i)], qs[i], _NT, preferred_element_type=F32)

    ones_rows = jnp.ones((acc_ref.shape[1] - LANES, tk), BF16)

    def pv_and_sum(kb, i, p):
        v_aug = jnp.concatenate([vt_ref[0, kb, head(i), :], ones_rows], axis=0)
        return jnp.dot(v_aug, p, preferred_element_type=F32)

    first_prev = jnp.maximum(kb0 - 1, 0)
    for i in range(n_streams):
        s_d = [jnp.where(row + j * tk <= col, scores(kb0 + j, i), -jnp.inf) for j in range(nd)]
        m = functools.reduce(jnp.maximum, [jnp.max(s, axis=0, keepdims=True) for s in s_d])
        p_d = [jnp.exp2(s - m).astype(BF16) for s in s_d]
        m_ref[i] = m
        acc = jnp.zeros(acc_ref.shape[1:], F32)
        for j in range(1, nd):
            acc = acc + pv_and_sum(kb0 + j, i, p_d[j])
        acc_ref[i] = acc
        p_ref[i] = p_d[0]
        s_ref[i] = scores(first_prev, i)

    def body(it, _):
        kb = kb0 - 1 - it
        nxt = jnp.maximum(kb - 1, 0)
        for i in range(n_streams):
            pv = pv_and_sum(kb + 1, i, p_ref[i])
            s_next = scores(nxt, i)
            s = s_ref[i]
            m_prev = m_ref[i]
            m_new = jnp.maximum(m_prev, jnp.max(s, axis=0, keepdims=True))
            alpha = jnp.exp2(m_prev - m_new)
            acc_ref[i] = alpha * (acc_ref[i] + pv)
            m_ref[i] = m_new
            p_ref[i] = jnp.exp2(s - m_new).astype(BF16)
            s_ref[i] = s_next
        return 0

    lax.fori_loop(0, kb0, body, 0)
    outs = []
    for i in range(n_streams):
        full = acc_ref[i] + pv_and_sum(0, i, p_ref[i])
        outs.append(full[:LANES] / full[LANES:LANES + 1])

    lam = (jnp.exp(jnp.sum(lq1_ref[...] * lk1_ref[...], axis=-1, keepdims=True))
           - jnp.exp(jnp.sum(lq2_ref[...] * lk2_ref[...], axis=-1, keepdims=True)) + lam_init)
    for h in range(n_streams // 2):
        o = outs[2 * h] - lam * outs[2 * h + 1]
        ms = jnp.mean(o * o, axis=0, keepdims=True)
        o_ref[0, :, h * LANES:(h + 1) * LANES] = (
            o * lax.rsqrt(ms + NORM_EPS) * sw_ref[...] * (1.0 - lam_init)).T


def _diff_attention(q, k, vt, lq1, lk1, lq2, lk2, subln_w, lam_init, batch, seq):
    t = ATT_TILE
    nk = seq // t
    db = q.shape[-1]
    d = HEAD_DIM
    dv = 2 * d
    assert dv == LANES
    tq = DIFF_Q_TILE
    assert seq % tq == 0 and tq % t == 0
    sw = jnp.broadcast_to(subln_w.astype(F32)[:, None], (dv, tq))
    vec = lambda a: a.astype(F32).reshape(1, d)
    kern = functools.partial(_diff_kernel, lam_init=lam_init)
    small = pl.BlockSpec((1, d), lambda bi, hi, qi: (0, 0))
    width = DIFF_STEP_LANES
    ns = 2 * (width // dv)
    assert db % width == 0
    q_spec, k_spec, vt_spec = _att_specs(tq, t, nk, width)
    return pl.pallas_call(
        kern,
        grid=(batch, db // width, seq // tq),
        in_specs=[q_spec, k_spec, vt_spec, small, small, small, small,
                  pl.BlockSpec((dv, tq), lambda bi, hi, qi: (0, 0))],
        out_specs=q_spec,
        out_shape=jax.ShapeDtypeStruct((batch, seq, db), F32),
        scratch_shapes=[pltpu.VMEM((ns, t, tq), F32), pltpu.VMEM((ns, t, tq), BF16),
                        pltpu.VMEM((ns, dv + BF16_SUBLANES, tq), F32), pltpu.VMEM((ns, 1, tq), F32)],
        compiler_params=_cparams(("arbitrary", "arbitrary", "arbitrary")),
        name="diff_attention",
    )(q.reshape(batch, seq, db), k.reshape(batch, nk, t, db), vt,
      vec(lq1), vec(lk1), vec(lq2), vec(lk2), sw)


def _gelu_tanh(x):
    return 0.5 * x * (1.0 + jnp.tanh(math.sqrt(2.0 / math.pi) * (x + 0.044715 * (x * x * x))))


def _sigmoid(x):
    return 0.5 * jnp.tanh(0.5 * x) + 0.5


def _silu(x):
    hx = 0.5 * x
    return hx * jnp.tanh(hx) + hx


def _merge_kernel(x_ref, y0_ref, y1_ref, y2_ref, y3_ref, osb_ref, odf_ref, g_ssm_ref, g_sb_ref,
                  g_df_ref, ml0_ref, ml1_ref, ml2_ref, wglu_ref, bglu_ref, bm_ref, wbr_ref, wout_ref,
                  fw_ref, o_ref, *, final_norm):
    d = x_ref.shape[-1]
    y = jnp.concatenate([r[0] for r in (y0_ref, y1_ref, y2_ref, y3_ref)], axis=1)
    y = _gelu_tanh(y)
    glu = jnp.dot(y.astype(BF16), wglu_ref[...], preferred_element_type=F32) + bglu_ref[...]
    branches = (y * _sigmoid(glu) * _silu(g_ssm_ref[...].astype(F32)),
                osb_ref[...] * _silu(g_sb_ref[...].astype(F32)),
                odf_ref[...] * _silu(g_df_ref[...].astype(F32)))
    logits = (ml0_ref, ml1_ref, ml2_ref)
    merged = None
    for n in range(3):
        gate = _sigmoid(logits[n][...].astype(F32) + bm_ref[:, n * d:(n + 1) * d])
        term = gate * jnp.dot(branches[n].astype(BF16), wbr_ref[n], preferred_element_type=F32)
        merged = term if merged is None else merged + term
    out = x_ref[...] + jnp.dot(merged.astype(BF16), wout_ref[...], preferred_element_type=F32)
    if final_norm:
        ms = jnp.mean(out * out, axis=-1, keepdims=True)
        out = out * lax.rsqrt(ms + NORM_EPS) * fw_ref[...]
    o_ref[...] = out


def _merge(x2d, yj, o_sb, o_df, rest, w_glu, b_glu, b_merge, w_branch, w_out, final_w,
           final_norm, *, tm=512):
    t, d = x2d.shape
    db = d // 2
    assert t % tm == 0 and yj.shape[0] == 4
    row = lambda i: (i, 0)
    const2 = lambda i: (0, 0)
    gate_tile0 = (3 * d) // db
    y_spec = lambda j: pl.BlockSpec((1, tm, LANES), lambda i: (j, i, 0))
    kern = functools.partial(_merge_kernel, final_norm=final_norm)
    return pl.pallas_call(
        kern,
        grid=(t // tm,),
        in_specs=[
            pl.BlockSpec((tm, d), row),
            y_spec(0), y_spec(1), y_spec(2), y_spec(3),
            pl.BlockSpec((tm, db), row),
            pl.BlockSpec((tm, db), row),
            pl.BlockSpec((tm, db), lambda i: (i, gate_tile0)),
            pl.BlockSpec((tm, db), lambda i: (i, gate_tile0 + 1)),
            pl.BlockSpec((tm, db), lambda i: (i, gate_tile0 + 2)),
            pl.BlockSpec((tm, d), lambda i: (i, 0)),
            pl.BlockSpec((tm, d), lambda i: (i, 1)),
            pl.BlockSpec((tm, d), lambda i: (i, 2)),
            pl.BlockSpec((db, db), const2),
            pl.BlockSpec((1, db), const2),
            pl.BlockSpec((1, 3 * d), const2),
            pl.BlockSpec((3, db, d), lambda i: (0, 0, 0)),
            pl.BlockSpec((d, d), const2),
            pl.BlockSpec((1, d), const2),
        ],
        out_specs=pl.BlockSpec((tm, d), row),
        out_shape=jax.ShapeDtypeStruct((t, d), F32),
        compiler_params=_cparams(("arbitrary",)),
        name="merge_out",
    )(x2d, yj, yj, yj, yj, o_sb, o_df, rest, rest, rest, rest, rest, rest,
      w_glu, b_glu.reshape(1, db).astype(F32), b_merge.reshape(1, 3 * d).astype(F32),
      w_branch, w_out, final_w.reshape(1, d).astype(F32))


def _layer(x, layer_idx, cos_t, sin_t, norm_w, w_in, b_merge, a_re, a_im, log_dt, b_re, b_im,
           c_re, c_im, d_skip, w_glu, b_glu, lq1, lk1, lq2, lk2, subln_w, w_branch, w_out,
           final_w, final_norm):
    mats = _ssm_matrices(a_re, a_im, log_dt, b_re, b_im, c_re, c_im, d_skip)
    return _layer_core(x, layer_idx, 0, cos_t, sin_t, norm_w, w_in[None].astype(BF16), b_merge, mats,
                       w_glu.astype(BF16), b_glu, lq1, lk1, lq2, lk2, subln_w, w_branch.astype(BF16),
                       w_out.astype(BF16), final_w, final_norm)


def _layer_core(x, layer_idx, w_slot, cos_t, sin_t, norm_w, w_bf, b_merge, mats, w_glu, b_glu,
                lq1, lk1, lq2, lk2, subln_w, w_branch, w_out, final_w, final_norm):
    b, s, d = x.shape
    db = d // 2
    x2d = x.reshape(b * s, d)
    rest, uj, sbq, sbk, dfq, dfk, sbvt, dfvt = _in_proj(x2d, norm_w, w_bf, w_slot, cos_t, sin_t, b, s)
    yj = _ssm(uj, mats, b, s)
    o_sb = _stick_breaking(sbq, sbk, sbvt, b, s)
    lam_init = 0.8 - 0.6 * math.exp(-0.3 * layer_idx)
    o_df = _diff_attention(dfq, dfk, dfvt, lq1, lk1, lq2, lk2, subln_w, lam_init, b, s)

    out = _merge(x2d, yj, o_sb.reshape(b * s, db), o_df.reshape(b * s, db),
                 rest, w_glu, b_glu, b_merge, w_branch, w_out, final_w, final_norm)
    return out.reshape(b, s, d)


def kernel(x, norm_w, w_in, b_merge, ssm_a_re, ssm_a_im, ssm_log_dt, ssm_b_re, ssm_b_im, ssm_c_re,
           ssm_c_im, ssm_d, ssm_w_glu, ssm_b_glu, diff_lq1, diff_lk1, diff_lq2, diff_lk2,
           diff_subln_w, w_branch, w_out, final_norm_w):
    depth = norm_w.shape[0]
    seq = x.shape[1]
    db = x.shape[2] // 2
    cos_t, sin_t = _rope_tables(seq, db)
    w_bf = w_in.astype(BF16)
    mats = jax.vmap(_ssm_matrices)(ssm_a_re, ssm_a_im, ssm_log_dt, ssm_b_re, ssm_b_im, ssm_c_re,
                                   ssm_c_im, ssm_d)
    w_glu, w_br, w_o = ssm_w_glu.astype(BF16), w_branch.astype(BF16), w_out.astype(BF16)
    for i in range(depth):
        x = _layer_core(x, i, i, cos_t, sin_t, norm_w[i], w_bf, b_merge[i],
                        [m[i] for m in mats], w_glu[i], ssm_b_glu[i], diff_lq1[i], diff_lk1[i],
                        diff_lq2[i], diff_lk2[i], diff_subln_w[i], w_br[i], w_o[i], final_norm_w,
                        i == depth - 1)
    return x
```

```python
import functools
import math

import jax
import jax.numpy as jnp
from jax import lax
from jax.experimental import pallas as pl
from jax.experimental.pallas import tpu as pltpu

F32 = jnp.float32
BF16 = jnp.bfloat16

NORM_EPS = 1e-6
ROPE_THETA = 10000.0
HEAD_DIM = 64
N_PROJ_TILES = 16
(COL_SSM_U, COL_SSM_GATE, COL_SB_Q, COL_SB_K, COL_SB_V, COL_SB_GATE,
 COL_DF_Q, COL_DF_K, COL_DF_V, COL_DF_GATE, COL_LOGITS) = range(11)
SSM_GROUP = 16
SSM_STATE = 64
SSM_CHUNK = 16
SUBLANES = 8
BF16_SUBLANES = 16
LANES = 128
ATT_TILE = 256
DIFF_Q_TILE = 512
IN_PROJ_TILE = 512
SB_STEP_LANES = 512
DIFF_STEP_LANES = 512
VMEM_LIMIT = 56 * 1024 * 1024
LOG2E = 1.4426950408889634
SB_EXIT_COST = 120.0

_NT = (((1,), (1,)), ((), ()))


def _cparams(sem):
    return pltpu.CompilerParams(dimension_semantics=sem, vmem_limit_bytes=VMEM_LIMIT)


def _in_proj_kernel(x_ref, nw_ref, w_ref, cos_ref, sin_ref,
                    rest_ref, uj_ref, sbq_ref, sbk_ref, dfq_ref, dfk_ref, sbvt_ref, dfvt_ref, *,
                    sb_scale, df_scale):
    t = ATT_TILE
    db = sbq_ref.shape[-1]

    def proj(lhs, tile, n_tiles=1):
        return jnp.dot(lhs, w_ref[:, tile * db:(tile + n_tiles) * db], preferred_element_type=F32)

    x = x_ref[...]
    ms = jnp.mean(x * x, axis=-1, keepdims=True)
    h = (x * lax.rsqrt(ms + NORM_EPS) * nw_ref[...]).astype(BF16)

    r = lax.broadcasted_iota(jnp.int32, (t, t), 0)
    c = lax.broadcasted_iota(jnp.int32, (t, t), 1)
    perm = jnp.where((r % SUBLANES) * (t // SUBLANES) + r // SUBLANES == c, 1.0, 0.0).astype(BF16)
    n_blocks = x.shape[0] // t
    hperm = jnp.concatenate(
        [jnp.dot(perm, h[b * t:(b + 1) * t], preferred_element_type=F32).astype(BF16)
         for b in range(n_blocks)], axis=0)

    n_logit = N_PROJ_TILES - COL_LOGITS
    for c0 in range(0, n_logit, 3):
        rest_ref[:, c0 * db:(c0 + 3) * db] = proj(h, COL_LOGITS + c0, 3).astype(rest_ref.dtype)
    for n, tile in enumerate((COL_SSM_GATE, COL_SB_GATE, COL_DF_GATE)):
        rest_ref[:, (n_logit + n) * db:(n_logit + n + 1) * db] = proj(h, tile).astype(rest_ref.dtype)

    u = proj(h, COL_SSM_U)
    for j in range(db // LANES):
        uj_ref[j] = u[:, j * LANES:(j + 1) * LANES]

    def rope(a):
        lane = lax.broadcasted_iota(jnp.int32, a.shape, 1)
        first_half = (lane % HEAD_DIM) < (HEAD_DIM // 2)
        swapped = jnp.where(first_half,
                            pltpu.roll(a, db - HEAD_DIM // 2, 1),
                            pltpu.roll(a, HEAD_DIM // 2, 1))
        return a * cos_ref[...] + swapped * sin_ref[...]

    sbq_ref[...] = (proj(h, COL_SB_Q) * sb_scale).astype(BF16)
    sbk_ref[...] = proj(hperm, COL_SB_K).astype(BF16)
    dfq_ref[...] = (rope(proj(h, COL_DF_Q)) * df_scale).astype(BF16)
    dfk_ref[...] = rope(proj(h, COL_DF_K)).astype(BF16)
    sbv = proj(hperm, COL_SB_V).astype(BF16)
    dfv = proj(h, COL_DF_V).astype(BF16)
    for b in range(n_blocks):
        sbvt_ref[0, b] = sbv[b * t:(b + 1) * t].T
        dfvt_ref[0, b] = dfv[b * t:(b + 1) * t].T


def _in_proj(x2d, norm_w, w_bf, layer, cos_t, sin_t, batch, seq):
    t_rows, d = x2d.shape
    db = d // 2
    tm = IN_PROJ_TILE
    t = ATT_TILE
    assert seq % tm == 0 and tm % t == 0 and w_bf.shape[1:] == (d, N_PROJ_TILES * db)
    nk = seq // tm
    n_rest = (N_PROJ_TILES - COL_LOGITS + 3) * db
    scale = 1.0 / math.sqrt(HEAD_DIM)
    kern = functools.partial(_in_proj_kernel, sb_scale=scale, df_scale=scale * LOG2E)
    row = lambda i: (i, 0)
    layer_weight = pl.BlockSpec((None,) + w_bf.shape[1:], lambda i: (layer, 0, 0),
                                pipeline_mode=pl.Buffered(1))
    qk_sds = jax.ShapeDtypeStruct((t_rows, db), BF16)
    vt_sds = jax.ShapeDtypeStruct((batch, seq // t, db, t), BF16)
    vt_spec = pl.BlockSpec((1, tm // t, db, t), lambda i: (i // nk, i % nk, 0, 0))
    nj = db // LANES
    uj_sds = jax.ShapeDtypeStruct((nj, t_rows, LANES), F32)
    uj_spec = pl.BlockSpec((nj, tm, LANES), lambda i: (0, i, 0))
    return pl.pallas_call(
        kern,
        grid=(t_rows // tm,),
        in_specs=[
            pl.BlockSpec((tm, d), row),
            pl.BlockSpec((1, d), lambda i: (0, 0)),
            layer_weight,
            pl.BlockSpec((tm, db), lambda i: (i % nk, 0)),
            pl.BlockSpec((tm, db), lambda i: (i % nk, 0)),
        ],
        out_specs=[
            pl.BlockSpec((tm, n_rest), row), uj_spec,
            pl.BlockSpec((tm, db), row), pl.BlockSpec((tm, db), row),
            pl.BlockSpec((tm, db), row), pl.BlockSpec((tm, db), row),
            vt_spec, vt_spec,
        ],
        out_shape=[jax.ShapeDtypeStruct((t_rows, n_rest), BF16), uj_sds, qk_sds, qk_sds, qk_sds, qk_sds,
                   vt_sds, vt_sds],
        compiler_params=_cparams(("arbitrary",)),
        name="in_proj",
    )(x2d, norm_w.reshape(1, d), w_bf, cos_t, sin_t)


def _rope_tables(seq, width):
    half = HEAD_DIM // 2
    inv_freq = ROPE_THETA ** (-jnp.arange(half, dtype=F32) / half)
    ang = jnp.arange(seq, dtype=jnp.int32).astype(F32)[:, None] * inv_freq[None, :]
    cos, sin = jnp.cos(ang), jnp.sin(ang)
    reps = width // HEAD_DIM
    cos_t = jnp.tile(jnp.concatenate([cos, cos], axis=-1), (1, reps))
    sin_t = jnp.tile(jnp.concatenate([-sin, sin], axis=-1), (1, reps))
    return cos_t, sin_t


SSM_TILE_GROUPS = LANES // SSM_GROUP
SSM_TILE_STATES = SSM_TILE_GROUPS * SSM_STATE


def _expand_groups(a, rows_per_group, cols_per_group, col_groups):
    tg = SSM_TILE_GROUPS
    full = jnp.concatenate([a] * tg, axis=0)
    row = lax.broadcasted_iota(jnp.int32, full.shape, 0)
    col = lax.broadcasted_iota(jnp.int32, full.shape, 1)
    same = row // rows_per_group == (col // cols_per_group) % col_groups
    return jnp.where(same, full, 0.0).astype(BF16)


def _ssm_kernel(u_ref, kc_ref, wre_ref, wim_ref, vre_ref, vim_ref, lam_ref, d_ref, y_ref,
                t_ref, w_ref, v_ref, s_ref, x_ref, *, n_chunks):
    ns = SSM_TILE_STATES
    L = SSM_CHUNK
    tg = SSM_TILE_GROUPS

    @pl.when(pl.program_id(1) == 0)
    def _():
        t_ref[...] = jnp.zeros_like(t_ref)
        for lag in range(L):
            blk = _expand_groups(kc_ref[0, lag], SSM_GROUP, SSM_GROUP, tg)
            for s in range(L - lag):
                t_ref[s * LANES:(s + 1) * LANES, (s + lag) * LANES:(s + lag + 1) * LANES] = blk
        for s in range(L):
            rows = slice(s * LANES, (s + 1) * LANES)
            w_ref[rows, :ns] = _expand_groups(wre_ref[0, L - 1 - s], SSM_GROUP, SSM_STATE, tg)
            w_ref[rows, ns:] = _expand_groups(wim_ref[0, L - 1 - s], SSM_GROUP, SSM_STATE, tg)
        v_ref[:ns, :] = _expand_groups(vre_ref[0], SSM_STATE, SSM_GROUP, tg)
        v_ref[ns:, :] = _expand_groups(vim_ref[0], SSM_STATE, SSM_GROUP, tg)

    u = jnp.concatenate([u_ref[0, pl.ds(s, n_chunks, stride=L), :] for s in range(L)], axis=1)
    ub = u.astype(BF16)
    s_ref[...] = jnp.dot(ub, w_ref[...], preferred_element_type=F32)
    lam_r = lam_ref[0, :, :ns]
    lam_i = lam_ref[0, :, ns:]

    def step(k, carry):
        xr, xi = carry
        row = pl.ds(k, 1)
        x_ref[row, :ns] = xr
        x_ref[row, ns:] = xi
        nxr = lam_r * xr - lam_i * xi + s_ref[row, :ns]
        nxi = lam_r * xi + lam_i * xr + s_ref[row, ns:]
        return nxr, nxi

    zero = jnp.zeros((1, ns), F32)
    lax.fori_loop(0, n_chunks, step, (zero, zero))

    cols = 2 * LANES
    y = jnp.concatenate(
        [jnp.dot(ub[:, :c + cols], t_ref[:c + cols, c:c + cols], preferred_element_type=F32)
         for c in range(0, L * LANES, cols)], axis=1) + d_ref[0] * u
    y = y + jnp.dot(x_ref[...].astype(BF16), v_ref[...], preferred_element_type=F32)
    for s in range(L):
        y_ref[0, pl.ds(s, n_chunks, stride=L), :] = y[:, s * LANES:(s + 1) * LANES]


def _ssm_matrices(a_re, a_im, log_dt, b_re, b_im, c_re, c_im, d_skip):
    g, p, c = b_re.shape
    L = SSM_CHUNK
    tg = SSM_TILE_GROUPS
    nj = g // tg
    dt = jnp.exp(log_dt.astype(F32))[:, None]
    mag = jnp.exp(dt * a_re)
    abar_re = mag * jnp.cos(dt * a_im)
    abar_im = mag * jnp.sin(dt * a_im)
    denom = a_re * a_re + a_im * a_im
    coef_re = ((abar_re - 1.0) * a_re + abar_im * a_im) / denom
    coef_im = (abar_im * a_re - (abar_re - 1.0) * a_im) / denom
    bb_re = coef_re[..., None] * b_re - coef_im[..., None] * b_im
    bb_im = coef_re[..., None] * b_im + coef_im[..., None] * b_re
    tau = jnp.arange(L + 1, dtype=F32)[:, None, None]
    pw_mag = jnp.exp(tau * (dt * a_re)[None])
    pw_re = pw_mag * jnp.cos(tau * (dt * a_im)[None])
    pw_im = pw_mag * jnp.sin(tau * (dt * a_im)[None])
    m_re = pw_re[..., None] * bb_re[None] - pw_im[..., None] * bb_im[None]
    m_im = pw_re[..., None] * bb_im[None] + pw_im[..., None] * bb_re[None]
    hi = lax.Precision.HIGHEST
    kern = (jnp.einsum('gcp,tgpd->tgcd', c_re, m_re[:L], precision=hi)
            - jnp.einsum('gcp,tgpd->tgcd', c_im, m_im[:L], precision=hi))
    kc = kern.transpose(0, 3, 1, 2).reshape(L, c, nj, tg * c).transpose(2, 0, 1, 3)

    def state_in(m):
        return m[:L].transpose(0, 3, 1, 2).reshape(L, c, nj, tg * p).transpose(2, 0, 1, 3)

    pr, pi = pw_re[1:], pw_im[1:]
    v_re = (c_re[None] * pr[:, :, None, :] - c_im[None] * pi[:, :, None, :])
    v_im = -(c_re[None] * pi[:, :, None, :] + c_im[None] * pr[:, :, None, :])

    def state_out(v):
        return v.transpose(3, 0, 1, 2).reshape(p, L, nj, tg * c).transpose(2, 0, 1, 3).reshape(
            nj, p, L * LANES)

    lam = jnp.concatenate([pw_re[L].reshape(nj, 1, tg * p), pw_im[L].reshape(nj, 1, tg * p)], axis=-1)
    d_t = jnp.broadcast_to(d_skip.astype(F32).reshape(nj, 1, 1, tg * c), (nj, 1, L, tg * c))
    return (kc, state_in(m_re), state_in(m_im), state_out(v_re), state_out(v_im), lam,
            d_t.reshape(nj, 1, L * LANES))


def _ssm(uj, mats, batch, seq):
    nj, rows, lanes = uj.shape
    L = SSM_CHUNK
    assert lanes == LANES and seq % L == 0 and rows == batch * seq
    nc = seq // L
    width = L * LANES
    ns2 = 2 * SSM_TILE_STATES
    kc, w_re, w_im, v_re, v_im, lam, d_t = mats
    kern = functools.partial(_ssm_kernel, n_chunks=nc)

    def per_tile(a):
        nd = a.ndim - 1
        return pl.BlockSpec((1,) + a.shape[1:], lambda j, b: (j,) + (0,) * nd)

    act = pl.BlockSpec((1, seq, LANES), lambda j, b: (j, b, 0))
    params = (kc, w_re, w_im, v_re, v_im, lam, d_t)
    return pl.pallas_call(
        kern,
        grid=(nj, batch),
        in_specs=[act] + [per_tile(a) for a in params],
        out_specs=act,
        out_shape=jax.ShapeDtypeStruct((nj, rows, LANES), F32),
        scratch_shapes=[pltpu.VMEM((width, width), BF16), pltpu.VMEM((width, ns2), BF16),
                        pltpu.VMEM((ns2, width), BF16),
                        pltpu.VMEM((nc, ns2), F32), pltpu.VMEM((nc, ns2), F32)],
        compiler_params=_cparams(("arbitrary", "arbitrary")),
        name="ssm_scan",
    )(uj, *params)


def _split_halves(q_ref):
    out = []
    for p in range(q_ref.shape[-1] // LANES):
        qf = q_ref[0, :, p * LANES:(p + 1) * LANES].astype(F32)
        lane = lax.broadcasted_iota(jnp.int32, qf.shape, 1)
        out += [jnp.where((lane // HEAD_DIM) == i, qf, 0.0).astype(BF16) for i in range(2)]
    return out


def _att_specs(tq, t, nk, width=LANES):
    q_spec = pl.BlockSpec((1, tq, width), lambda bi, pi, qi: (bi, qi, pi))
    k_spec = pl.BlockSpec((1, nk, t, width), lambda bi, pi, qi: (bi, 0, 0, pi))
    vt_spec = pl.BlockSpec((1, nk, width, t), lambda bi, pi, qi: (bi, 0, pi, 0))
    return q_spec, k_spec, vt_spec


def _sb_kernel(q_ref, k_ref, vt_ref, o_ref, acc_ref, carry_ref):
    tk = tq = ATT_TILE
    d = HEAD_DIM
    seg = tk // SUBLANES
    qi = pl.program_id(2)
    qs = _split_halves(q_ref)
    n_heads = len(qs)
    row = lax.broadcasted_iota(jnp.int32, (tk, tq), 0)
    col = lax.broadcasted_iota(jnp.int32, (tk, tq), 1)
    diag_mask = (row % SUBLANES) * seg + row // SUBLANES < col
    rid = lax.broadcasted_iota(jnp.int32, (SUBLANES, tq), 0)

    def scores(kb, hh):
        pair = slice((hh // 2) * LANES, (hh // 2 + 1) * LANES)
        return lax.dot_general(k_ref[0, kb, :, pair], qs[hh], _NT, preferred_element_type=F32)

    def weights(z, carry, masked):
        cost = jnp.maximum(z, 0.0) + jnp.log(1.0 + jnp.exp(-jnp.abs(z)))
        if masked:
            cost = jnp.where(diag_mask, cost, 0.0)
        run = jnp.zeros((SUBLANES, tq), F32)
        incl = [None] * seg
        for a in reversed(range(seg)):
            run = run + cost[a * SUBLANES:(a + 1) * SUBLANES]
            incl[a] = run
        offset = jnp.zeros((SUBLANES, tq), F32)
        for r in range(1, SUBLANES):
            offset = offset + jnp.where(rid < r, jnp.broadcast_to(run[r:r + 1], (SUBLANES, tq)), 0.0)
        base = offset if carry is None else offset + carry
        w = jnp.concatenate(
            [jnp.exp((z[a * SUBLANES:(a + 1) * SUBLANES] - incl[a]) - base) for a in range(seg)],
            axis=0)
        if masked:
            w = jnp.where(diag_mask, w, 0.0)
        return w.astype(BF16), jnp.sum(run, axis=0, keepdims=True)

    prev = jnp.maximum(qi - 1, 0)
    no_prev_cost = jnp.where(qi == 0, jnp.inf, 0.0).astype(F32)
    for hh in range(n_heads):
        rows = slice(hh * d, (hh + 1) * d)
        z_diag = scores(qi, hh)
        z_prev = scores(prev, hh)
        w_diag, tot_diag = weights(z_diag, None, True)
        w_prev, tot_prev = weights(z_prev, tot_diag + no_prev_cost, False)
        acc_ref[rows, :] = (jnp.dot(vt_ref[0, qi, rows, :], w_diag, preferred_element_type=F32)
                            + jnp.dot(vt_ref[0, prev, rows, :], w_prev, preferred_element_type=F32))
        carry_ref[hh] = tot_diag + tot_prev

    def not_done():
        return jnp.min(carry_ref[...]) < SB_EXIT_COST

    def cond(state):
        kb, go = state
        return jnp.logical_and(kb >= 0, go)

    def body(state):
        kb, _ = state
        for hh in range(n_heads):
            rows = slice(hh * d, (hh + 1) * d)
            w, tot = weights(scores(kb, hh), carry_ref[hh], False)
            acc_ref[rows, :] += jnp.dot(vt_ref[0, kb, rows, :], w, preferred_element_type=F32)
            carry_ref[hh] += tot
        return kb - 1, not_done()

    lax.while_loop(cond, body, (qi - 2, not_done()))
    o_ref[0] = acc_ref[...].T


def _stick_breaking(q, k, vt, batch, seq):
    t = ATT_TILE
    nk = seq // t
    db = q.shape[-1]
    width = SB_STEP_LANES
    assert db % width == 0
    q_spec, k_spec, vt_spec = _att_specs(t, t, nk, width)
    return pl.pallas_call(
        _sb_kernel,
        grid=(batch, db // width, nk),
        in_specs=[q_spec, k_spec, vt_spec],
        out_specs=q_spec,
        out_shape=jax.ShapeDtypeStruct((batch, seq, db), F32),
        scratch_shapes=[pltpu.VMEM((width, t), F32), pltpu.VMEM((width // HEAD_DIM, 1, t), F32)],
        compiler_params=_cparams(("arbitrary", "arbitrary", "arbitrary")),
        name="stick_breaking",
    )(q.reshape(batch, seq, db), k.reshape(batch, nk, t, db), vt)


def _diff_kernel(q_ref, k_ref, vt_ref, lq1_ref, lk1_ref, lq2_ref, lk2_ref, sw_ref, o_ref,
                 s_ref, p_ref, acc_ref, m_ref, *, lam_init):
    tk = ATT_TILE
    tq = q_ref.shape[1]
    nd = tq // tk
    qi = pl.program_id(2)
    qs = _split_halves(q_ref)
    n_streams = len(qs)
    row = lax.broadcasted_iota(jnp.int32, (tk, tq), 0)
    col = lax.broadcasted_iota(jnp.int32, (tk, tq), 1)
    kb0 = qi * nd
    head = lambda i: slice((i // 2) * LANES, (i // 2 + 1) * LANES)

    def scores(kb, i):
        return lax.dot_general(k_ref[0, kb, :, head(i)], qs[i], _NT, preferred_element_type=F32)

    ones_rows = jnp.ones((acc_ref.shape[1] - LANES, tk), BF16)

    def pv_and_sum(kb, i, p):
        v_aug = jnp.concatenate([vt_ref[0, kb, head(i), :], ones_rows], axis=0)
        return jnp.dot(v_aug, p, preferred_element_type=F32)

    first_prev = jnp.maximum(kb0 - 1, 0)
    for i in range(n_streams):
        s_d = [jnp.where(row + j * tk <= col, scores(kb0 + j, i), -jnp.inf) for j in range(nd)]
        m = functools.reduce(jnp.maximum, [jnp.max(s, axis=0, keepdims=True) for s in s_d])
        p_d = [jnp.exp2(s - m).astype(BF16) for s in s_d]
        m_ref[i] = m
        acc = jnp.zeros(acc_ref.shape[1:], F32)
        for j in range(1, nd):
            acc = acc + pv_and_sum(kb0 + j, i, p_d[j])
        acc_ref[i] = acc
        p_ref[i] = p_d[0]
        s_ref[i] = scores(first_prev, i)

    def body(it, _):
        kb = kb0 - 1 - it
        nxt = jnp.maximum(kb - 1, 0)
        for i in range(n_streams):
            pv = pv_and_sum(kb + 1, i, p_ref[i])
            s_next = scores(nxt, i)
            s = s_ref[i]
            m_prev = m_ref[i]
            m_new = jnp.maximum(m_prev, jnp.max(s, axis=0, keepdims=True))
            alpha = jnp.exp2(m_prev - m_new)
            acc_ref[i] = alpha * (acc_ref[i] + pv)
            m_ref[i] = m_new
            p_ref[i] = jnp.exp2(s - m_new).astype(BF16)
            s_ref[i] = s_next
        return 0

    lax.fori_loop(0, kb0, body, 0)
    outs = []
    for i in range(n_streams):
        full = acc_ref[i] + pv_and_sum(0, i, p_ref[i])
        outs.append(full[:LANES] / full[LANES:LANES + 1])

    lam = (jnp.exp(jnp.sum(lq1_ref[...] * lk1_ref[...], axis=-1, keepdims=True))
           - jnp.exp(jnp.sum(lq2_ref[...] * lk2_ref[...], axis=-1, keepdims=True)) + lam_init)
    for h in range(n_streams // 2):
        o = outs[2 * h] - lam * outs[2 * h + 1]
        ms = jnp.mean(o * o, axis=0, keepdims=True)
        o_ref[0, :, h * LANES:(h + 1) * LANES] = (
            o * lax.rsqrt(ms + NORM_EPS) * sw_ref[...] * (1.0 - lam_init)).T


def _diff_attention(q, k, vt, lq1, lk1, lq2, lk2, subln_w, lam_init, batch, seq):
    t = ATT_TILE
    nk = seq // t
    db = q.shape[-1]
    d = HEAD_DIM
    dv = 2 * d
    assert dv == LANES
    tq = DIFF_Q_TILE
    assert seq % tq == 0 and tq % t == 0
    sw = jnp.broadcast_to(subln_w.astype(F32)[:, None], (dv, tq))
    vec = lambda a: a.astype(F32).reshape(1, d)
    kern = functools.partial(_diff_kernel, lam_init=lam_init)
    small = pl.BlockSpec((1, d), lambda bi, hi, qi: (0, 0))
    width = DIFF_STEP_LANES
    ns = 2 * (width // dv)
    assert db % width == 0
    q_spec, k_spec, vt_spec = _att_specs(tq, t, nk, width)
    return pl.pallas_call(
        kern,
        grid=(batch, db // width, seq // tq),
        in_specs=[q_spec, k_spec, vt_spec, small, small, small, small,
                  pl.BlockSpec((dv, tq), lambda bi, hi, qi: (0, 0))],
        out_specs=q_spec,
        out_shape=jax.ShapeDtypeStruct((batch, seq, db), F32),
        scratch_shapes=[pltpu.VMEM((ns, t, tq), F32), pltpu.VMEM((ns, t, tq), BF16),
                        pltpu.VMEM((ns, dv + BF16_SUBLANES, tq), F32), pltpu.VMEM((ns, 1, tq), F32)],
        compiler_params=_cparams(("arbitrary", "arbitrary", "arbitrary")),
        name="diff_attention",
    )(q.reshape(batch, seq, db), k.reshape(batch, nk, t, db), vt,
      vec(lq1), vec(lk1), vec(lq2), vec(lk2), sw)


def _gelu_tanh(x):
    return 0.5 * x * (1.0 + jnp.tanh(math.sqrt(2.0 / math.pi) * (x + 0.044715 * (x * x * x))))


def _sigmoid(x):
    return 0.5 * jnp.tanh(0.5 * x) + 0.5


def _silu(x):
    hx = 0.5 * x
    return hx * jnp.tanh(hx) + hx


def _merge_kernel(x_ref, y0_ref, y1_ref, y2_ref, y3_ref, osb_ref, odf_ref, g_ssm_ref, g_sb_ref,
                  g_df_ref, ml0_ref, ml1_ref, ml2_ref, wglu_ref, bglu_ref, bm_ref, wbr_ref, wout_ref,
                  fw_ref, o_ref, *, final_norm):
    d = x_ref.shape[-1]
    y = jnp.concatenate([r[0] for r in (y0_ref, y1_ref, y2_ref, y3_ref)], axis=1)
    y = _gelu_tanh(y)
    glu = jnp.dot(y.astype(BF16), wglu_ref[...], preferred_element_type=F32) + bglu_ref[...]
    branches = (y * _sigmoid(glu) * _silu(g_ssm_ref[...].astype(F32)),
                osb_ref[...] * _silu(g_sb_ref[...].astype(F32)),
                odf_ref[...] * _silu(g_df_ref[...].astype(F32)))
    logits = (ml0_ref, ml1_ref, ml2_ref)
    merged = None
    for n in range(3):
        gate = _sigmoid(logits[n][...].astype(F32) + bm_ref[:, n * d:(n + 1) * d])
        term = gate * jnp.dot(branches[n].astype(BF16), wbr_ref[n], preferred_element_type=F32)
        merged = term if merged is None else merged + term
    out = x_ref[...] + jnp.dot(merged.astype(BF16), wout_ref[...], preferred_element_type=F32)
    if final_norm:
        ms = jnp.mean(out * out, axis=-1, keepdims=True)
        out = out * lax.rsqrt(ms + NORM_EPS) * fw_ref[...]
    o_ref[...] = out


def _merge(x2d, yj, o_sb, o_df, rest, w_glu, b_glu, b_merge, w_branch, w_out, final_w,
           final_norm, *, tm=512):
    t, d = x2d.shape
    db = d // 2
    assert t % tm == 0 and yj.shape[0] == 4
    row = lambda i: (i, 0)
    const2 = lambda i: (0, 0)
    gate_tile0 = (3 * d) // db
    y_spec = lambda j: pl.BlockSpec((1, tm, LANES), lambda i: (j, i, 0))
    kern = functools.partial(_merge_kernel, final_norm=final_norm)
    return pl.pallas_call(
        kern,
        grid=(t // tm,),
        in_specs=[
            pl.BlockSpec((tm, d), row),
            y_spec(0), y_spec(1), y_spec(2), y_spec(3),
            pl.BlockSpec((tm, db), row),
            pl.BlockSpec((tm, db), row),
            pl.BlockSpec((tm, db), lambda i: (i, gate_tile0)),
            pl.BlockSpec((tm, db), lambda i: (i, gate_tile0 + 1)),
            pl.BlockSpec((tm, db), lambda i: (i, gate_tile0 + 2)),
            pl.BlockSpec((tm, d), lambda i: (i, 0)),
            pl.BlockSpec((tm, d), lambda i: (i, 1)),
            pl.BlockSpec((tm, d), lambda i: (i, 2)),
            pl.BlockSpec((db, db), const2),
            pl.BlockSpec((1, db), const2),
            pl.BlockSpec((1, 3 * d), const2),
            pl.BlockSpec((3, db, d), lambda i: (0, 0, 0)),
            pl.BlockSpec((d, d), const2),
            pl.BlockSpec((1, d), const2),
        ],
        out_specs=pl.BlockSpec((tm, d), row),
        out_shape=jax.ShapeDtypeStruct((t, d), F32),
        compiler_params=_cparams(("arbitrary",)),
        name="merge_out",
    )(x2d, yj, yj, yj, yj, o_sb, o_df, rest, rest, rest, rest, rest, rest,
      w_glu, b_glu.reshape(1, db).astype(F32), b_merge.reshape(1, 3 * d).astype(F32),
      w_branch, w_out, final_w.reshape(1, d).astype(F32))


def _layer(x, layer_idx, cos_t, sin_t, norm_w, w_in, b_merge, a_re, a_im, log_dt, b_re, b_im,
           c_re, c_im, d_skip, w_glu, b_glu, lq1, lk1, lq2, lk2, subln_w, w_branch, w_out,
           final_w, final_norm):
    mats = _ssm_matrices(a_re, a_im, log_dt, b_re, b_im, c_re, c_im, d_skip)
    return _layer_core(x, layer_idx, 0, cos_t, sin_t, norm_w, w_in[None].astype(BF16), b_merge, mats,
                       w_glu.astype(BF16), b_glu, lq1, lk1, lq2, lk2, subln_w, w_branch.astype(BF16),
                       w_out.astype(BF16), final_w, final_norm)


def _layer_core(x, layer_idx, w_slot, cos_t, sin_t, norm_w, w_bf, b_merge, mats, w_glu, b_glu,
                lq1, lk1, lq2, lk2, subln_w, w_branch, w_out, final_w, final_norm):
    b, s, d = x.shape
    db = d // 2
    x2d = x.reshape(b * s, d)
    rest, uj, sbq, sbk, dfq, dfk, sbvt, dfvt = _in_proj(x2d, norm_w, w_bf, w_slot, cos_t, sin_t, b, s)
    yj = _ssm(uj, mats, b, s)
    o_sb = _stick_breaking(sbq, sbk, sbvt, b, s)
    lam_init = 0.8 - 0.6 * math.exp(-0.3 * layer_idx)
    o_df = _diff_attention(dfq, dfk, dfvt, lq1, lk1, lq2, lk2, subln_w, lam_init, b, s)

    out = _merge(x2d, yj, o_sb.reshape(b * s, db), o_df.reshape(b * s, db),
                 rest, w_glu, b_glu, b_merge, w_branch, w_out, final_w, final_norm)
    return out.reshape(b, s, d)


def kernel(x, norm_w, w_in, b_merge, ssm_a_re, ssm_a_im, ssm_log_dt, ssm_b_re, ssm_b_im, ssm_c_re,
           ssm_c_im, ssm_d, ssm_w_glu, ssm_b_glu, diff_lq1, diff_lk1, diff_lq2, diff_lk2,
           diff_subln_w, w_branch, w_out, final_norm_w):
    depth = norm_w.shape[0]
    seq = x.shape[1]
    db = x.shape[2] // 2
    cos_t, sin_t = _rope_tables(seq, db)
    w_bf = w_in.astype(BF16)
    mats = jax.vmap(_ssm_matrices)(ssm_a_re, ssm_a_im, ssm_log_dt, ssm_b_re, ssm_b_im, ssm_c_re,
                                   ssm_c_im, ssm_d)
    w_glu, w_br, w_o = ssm_w_glu.astype(BF16), w_branch.astype(BF16), w_out.astype(BF16)
    for i in range(depth):
        x = _layer_core(x, i, i, cos_t, sin_t, norm_w[i], w_bf, b_merge[i],
                        [m[i] for m in mats], w_glu[i], ssm_b_glu[i], diff_lq1[i], diff_lk1[i],
                        diff_lq2[i], diff_lk2[i], diff_subln_w[i], w_br[i], w_o[i], final_norm_w,
                        i == depth - 1)
    return x
```

```python
import functools
import math

import jax
import jax.numpy as jnp
from jax import lax
from jax.experimental import pallas as pl
from jax.experimental.pallas import tpu as pltpu

F32 = jnp.float32
BF16 = jnp.bfloat16

NORM_EPS = 1e-6
ROPE_THETA = 10000.0
HEAD_DIM = 64
N_PROJ_TILES = 16
(COL_SSM_U, COL_SSM_GATE, COL_SB_Q, COL_SB_K, COL_SB_V, COL_SB_GATE,
 COL_DF_Q, COL_DF_K, COL_DF_V, COL_DF_GATE, COL_LOGITS) = range(11)
SSM_GROUP = 16
SSM_STATE = 64
SSM_CHUNK = 16
SUBLANES = 8
BF16_SUBLANES = 16
LANES = 128
ATT_TILE = 256
DIFF_Q_TILE = 512
IN_PROJ_TILE = 512
SB_STEP_LANES = 512
DIFF_STEP_LANES = 512
VMEM_LIMIT = 56 * 1024 * 1024
LOG2E = 1.4426950408889634
SB_EXIT_COST = 120.0

_NT = (((1,), (1,)), ((), ()))


def _cparams(sem):
    return pltpu.CompilerParams(dimension_semantics=sem, vmem_limit_bytes=VMEM_LIMIT)


def _in_proj_kernel(x_ref, nw_ref, w_ref, cos_ref, sin_ref,
                    rest_ref, uj_ref, sbq_ref, sbk_ref, dfq_ref, dfk_ref, sbvt_ref, dfvt_ref, *,
                    sb_scale, df_scale):
    t = ATT_TILE
    db = sbq_ref.shape[-1]

    def proj(lhs, tile, n_tiles=1):
        return jnp.dot(lhs, w_ref[:, tile * db:(tile + n_tiles) * db], preferred_element_type=F32)

    x = x_ref[...]
    ms = jnp.mean(x * x, axis=-1, keepdims=True)
    h = (x * lax.rsqrt(ms + NORM_EPS) * nw_ref[...]).astype(BF16)

    r = lax.broadcasted_iota(jnp.int32, (t, t), 0)
    c = lax.broadcasted_iota(jnp.int32, (t, t), 1)
    perm = jnp.where((r % SUBLANES) * (t // SUBLANES) + r // SUBLANES == c, 1.0, 0.0).astype(BF16)
    n_blocks = x.shape[0] // t
    hperm = jnp.concatenate(
        [jnp.dot(perm, h[b * t:(b + 1) * t], preferred_element_type=F32).astype(BF16)
         for b in range(n_blocks)], axis=0)

    n_logit = N_PROJ_TILES - COL_LOGITS
    for c0 in range(0, n_logit, 3):
        rest_ref[:, c0 * db:(c0 + 3) * db] = proj(h, COL_LOGITS + c0, 3).astype(rest_ref.dtype)
    for n, tile in enumerate((COL_SSM_GATE, COL_SB_GATE, COL_DF_GATE)):
        rest_ref[:, (n_logit + n) * db:(n_logit + n + 1) * db] = proj(h, tile).astype(rest_ref.dtype)

    u = proj(h, COL_SSM_U)
    for j in range(db // LANES):
        uj_ref[j] = u[:, j * LANES:(j + 1) * LANES]

    def rope(a):
        lane = lax.broadcasted_iota(jnp.int32, a.shape, 1)
        first_half = (lane % HEAD_DIM) < (HEAD_DIM // 2)
        swapped = jnp.where(first_half,
                            pltpu.roll(a, db - HEAD_DIM // 2, 1),
                            pltpu.roll(a, HEAD_DIM // 2, 1))
        return a * cos_ref[...] + swapped * sin_ref[...]

    sbq_ref[...] = (proj(h, COL_SB_Q) * sb_scale).astype(BF16)
    sbk_ref[...] = proj(hperm, COL_SB_K).astype(BF16)
    dfq_ref[...] = (rope(proj(h, COL_DF_Q)) * df_scale).astype(BF16)
    dfk_ref[...] = rope(proj(h, COL_DF_K)).astype(BF16)
    sbv = proj(hperm, COL_SB_V).astype(BF16)
    dfv = proj(h, COL_DF_V).astype(BF16)
    for b in range(n_blocks):
        sbvt_ref[0, b] = sbv[b * t:(b + 1) * t].T
        dfvt_ref[0, b] = dfv[b * t:(b + 1) * t].T


def _in_proj(x2d, norm_w, w_bf, layer, cos_t, sin_t, batch, seq):
    t_rows, d = x2d.shape
    db = d // 2
    tm = IN_PROJ_TILE
    t = ATT_TILE
    assert seq % tm == 0 and tm % t == 0 and w_bf.shape[1:] == (d, N_PROJ_TILES * db)
    nk = seq // tm
    n_rest = (N_PROJ_TILES - COL_LOGITS + 3) * db
    scale = 1.0 / math.sqrt(HEAD_DIM)
    kern = functools.partial(_in_proj_kernel, sb_scale=scale, df_scale=scale * LOG2E)
    row = lambda i: (i, 0)
    layer_weight = pl.BlockSpec((None,) + w_bf.shape[1:], lambda i: (layer, 0, 0),
                                pipeline_mode=pl.Buffered(1))
    qk_sds = jax.ShapeDtypeStruct((t_rows, db), BF16)
    vt_sds = jax.ShapeDtypeStruct((batch, seq // t, db, t), BF16)
    vt_spec = pl.BlockSpec((1, tm // t, db, t), lambda i: (i // nk, i % nk, 0, 0))
    nj = db // LANES
    uj_sds = jax.ShapeDtypeStruct((nj, t_rows, LANES), F32)
    uj_spec = pl.BlockSpec((nj, tm, LANES), lambda i: (0, i, 0))
    return pl.pallas_call(
        kern,
        grid=(t_rows // tm,),
        in_specs=[
            pl.BlockSpec((tm, d), row),
            pl.BlockSpec((1, d), lambda i: (0, 0)),
            layer_weight,
            pl.BlockSpec((tm, db), lambda i: (i % nk, 0)),
            pl.BlockSpec((tm, db), lambda i: (i % nk, 0)),
        ],
        out_specs=[
            pl.BlockSpec((tm, n_rest), row), uj_spec,
            pl.BlockSpec((tm, db), row), pl.BlockSpec((tm, db), row),
            pl.BlockSpec((tm, db), row), pl.BlockSpec((tm, db), row),
            vt_spec, vt_spec,
        ],
        out_shape=[jax.ShapeDtypeStruct((t_rows, n_rest), BF16), uj_sds, qk_sds, qk_sds, qk_sds, qk_sds,
                   vt_sds, vt_sds],
        compiler_params=_cparams(("arbitrary",)),
        name="in_proj",
    )(x2d, norm_w.reshape(1, d), w_bf, cos_t, sin_t)


def _rope_tables(seq, width):
    half = HEAD_DIM // 2
    inv_freq = ROPE_THETA ** (-jnp.arange(half, dtype=F32) / half)
    ang = jnp.arange(seq, dtype=jnp.int32).astype(F32)[:, None] * inv_freq[None, :]
    cos, sin = jnp.cos(ang), jnp.sin(ang)
    reps = width // HEAD_DIM
    cos_t = jnp.tile(jnp.concatenate([cos, cos], axis=-1), (1, reps))
    sin_t = jnp.tile(jnp.concatenate([-sin, sin], axis=-1), (1, reps))
    return cos_t, sin_t


SSM_TILE_GROUPS = LANES // SSM_GROUP
SSM_TILE_STATES = SSM_TILE_GROUPS * SSM_STATE


def _expand_groups(a, rows_per_group, cols_per_group, col_groups):
    tg = SSM_TILE_GROUPS
    full = jnp.concatenate([a] * tg, axis=0)
    row = lax.broadcasted_iota(jnp.int32, full.shape, 0)
    col = lax.broadcasted_iota(jnp.int32, full.shape, 1)
    same = row // rows_per_group == (col // cols_per_group) % col_groups
    return jnp.where(same, full, 0.0).astype(BF16)


def _ssm_kernel(u_ref, kc_ref, wre_ref, wim_ref, vre_ref, vim_ref, lam_ref, d_ref, y_ref,
                t_ref, w_ref, v_ref, s_ref, x_ref, *, n_chunks):
    ns = SSM_TILE_STATES
    L = SSM_CHUNK
    tg = SSM_TILE_GROUPS

    @pl.when(pl.program_id(1) == 0)
    def _():
        t_ref[...] = jnp.zeros_like(t_ref)
        for lag in range(L):
            blk = _expand_groups(kc_ref[0, lag], SSM_GROUP, SSM_GROUP, tg)
            for s in range(L - lag):
                t_ref[s * LANES:(s + 1) * LANES, (s + lag) * LANES:(s + lag + 1) * LANES] = blk
        for s in range(L):
            rows = slice(s * LANES, (s + 1) * LANES)
            w_ref[rows, :ns] = _expand_groups(wre_ref[0, L - 1 - s], SSM_GROUP, SSM_STATE, tg)
            w_ref[rows, ns:] = _expand_groups(wim_ref[0, L - 1 - s], SSM_GROUP, SSM_STATE, tg)
        v_ref[:ns, :] = _expand_groups(vre_ref[0], SSM_STATE, SSM_GROUP, tg)
        v_ref[ns:, :] = _expand_groups(vim_ref[0], SSM_STATE, SSM_GROUP, tg)

    u = jnp.concatenate([u_ref[0, pl.ds(s, n_chunks, stride=L), :] for s in range(L)], axis=1)
    ub = u.astype(BF16)
    s_ref[...] = jnp.dot(ub, w_ref[...], preferred_element_type=F32)
    lam_r = lam_ref[0, :, :ns]
    lam_i = lam_ref[0, :, ns:]

    def step(k, carry):
        xr, xi = carry
        row = pl.ds(k, 1)
        x_ref[row, :ns] = xr
        x_ref[row, ns:] = xi
        nxr = lam_r * xr - lam_i * xi + s_ref[row, :ns]
        nxi = lam_r * xi + lam_i * xr + s_ref[row, ns:]
        return nxr, nxi

    zero = jnp.zeros((1, ns), F32)
    lax.fori_loop(0, n_chunks, step, (zero, zero), unroll=4)

    cols = 2 * LANES
    y = jnp.concatenate(
        [jnp.dot(ub[:, :c + cols], t_ref[:c + cols, c:c + cols], preferred_element_type=F32)
         for c in range(0, L * LANES, cols)], axis=1) + d_ref[0] * u
    y = y + jnp.dot(x_ref[...].astype(BF16), v_ref[...], preferred_element_type=F32)
    for s in range(L):
        y_ref[0, pl.ds(s, n_chunks, stride=L), :] = y[:, s * LANES:(s + 1) * LANES]


def _ssm_matrices(a_re, a_im, log_dt, b_re, b_im, c_re, c_im, d_skip):
    g, p, c = b_re.shape
    L = SSM_CHUNK
    tg = SSM_TILE_GROUPS
    nj = g // tg
    dt = jnp.exp(log_dt.astype(F32))[:, None]
    mag = jnp.exp(dt * a_re)
    abar_re = mag * jnp.cos(dt * a_im)
    abar_im = mag * jnp.sin(dt * a_im)
    denom = a_re * a_re + a_im * a_im
    coef_re = ((abar_re - 1.0) * a_re + abar_im * a_im) / denom
    coef_im = (abar_im * a_re - (abar_re - 1.0) * a_im) / denom
    bb_re = coef_re[..., None] * b_re - coef_im[..., None] * b_im
    bb_im = coef_re[..., None] * b_im + coef_im[..., None] * b_re
    tau = jnp.arange(L + 1, dtype=F32)[:, None, None]
    pw_mag = jnp.exp(tau * (dt * a_re)[None])
    pw_re = pw_mag * jnp.cos(tau * (dt * a_im)[None])
    pw_im = pw_mag * jnp.sin(tau * (dt * a_im)[None])
    m_re = pw_re[..., None] * bb_re[None] - pw_im[..., None] * bb_im[None]
    m_im = pw_re[..., None] * bb_im[None] + pw_im[..., None] * bb_re[None]
    hi = lax.Precision.HIGHEST
    kern = (jnp.einsum('gcp,tgpd->tgcd', c_re, m_re[:L], precision=hi)
            - jnp.einsum('gcp,tgpd->tgcd', c_im, m_im[:L], precision=hi))
    kc = kern.transpose(0, 3, 1, 2).reshape(L, c, nj, tg * c).transpose(2, 0, 1, 3)

    def state_in(m):
        return m[:L].transpose(0, 3, 1, 2).reshape(L, c, nj, tg * p).transpose(2, 0, 1, 3)

    pr, pi = pw_re[1:], pw_im[1:]
    v_re = (c_re[None] * pr[:, :, None, :] - c_im[None] * pi[:, :, None, :])
    v_im = -(c_re[None] * pi[:, :, None, :] + c_im[None] * pr[:, :, None, :])

    def state_out(v):
        return v.transpose(3, 0, 1, 2).reshape(p, L, nj, tg * c).transpose(2, 0, 1, 3).reshape(
            nj, p, L * LANES)

    lam = jnp.concatenate([pw_re[L].reshape(nj, 1, tg * p), pw_im[L].reshape(nj, 1, tg * p)], axis=-1)
    d_t = jnp.broadcast_to(d_skip.astype(F32).reshape(nj, 1, 1, tg * c), (nj, 1, L, tg * c))
    return (kc, state_in(m_re), state_in(m_im), state_out(v_re), state_out(v_im), lam,
            d_t.reshape(nj, 1, L * LANES))


def _ssm(uj, mats, batch, seq):
    nj, rows, lanes = uj.shape
    L = SSM_CHUNK
    assert lanes == LANES and seq % L == 0 and rows == batch * seq
    nc = seq // L
    width = L * LANES
    ns2 = 2 * SSM_TILE_STATES
    kc, w_re, w_im, v_re, v_im, lam, d_t = mats
    kern = functools.partial(_ssm_kernel, n_chunks=nc)

    def per_tile(a):
        nd = a.ndim - 1
        return pl.BlockSpec((1,) + a.shape[1:], lambda j, b: (j,) + (0,) * nd)

    act = pl.BlockSpec((1, seq, LANES), lambda j, b: (j, b, 0))
    params = (kc, w_re, w_im, v_re, v_im, lam, d_t)
    return pl.pallas_call(
        kern,
        grid=(nj, batch),
        in_specs=[act] + [per_tile(a) for a in params],
        out_specs=act,
        out_shape=jax.ShapeDtypeStruct((nj, rows, LANES), F32),
        scratch_shapes=[pltpu.VMEM((width, width), BF16), pltpu.VMEM((width, ns2), BF16),
                        pltpu.VMEM((ns2, width), BF16),
                        pltpu.VMEM((nc, ns2), F32), pltpu.VMEM((nc, ns2), F32)],
        compiler_params=_cparams(("arbitrary", "arbitrary")),
        name="ssm_scan",
    )(uj, *params)


def _split_halves(q_ref):
    out = []
    for p in range(q_ref.shape[-1] // LANES):
        qf = q_ref[0, :, p * LANES:(p + 1) * LANES].astype(F32)
        lane = lax.broadcasted_iota(jnp.int32, qf.shape, 1)
        out += [jnp.where((lane // HEAD_DIM) == i, qf, 0.0).astype(BF16) for i in range(2)]
    return out


def _att_specs(tq, t, nk, width=LANES):
    q_spec = pl.BlockSpec((1, tq, width), lambda bi, pi, qi: (bi, qi, pi))
    k_spec = pl.BlockSpec((1, nk, t, width), lambda bi, pi, qi: (bi, 0, 0, pi))
    vt_spec = pl.BlockSpec((1, nk, width, t), lambda bi, pi, qi: (bi, 0, pi, 0))
    return q_spec, k_spec, vt_spec


def _sb_kernel(q_ref, k_ref, vt_ref, o_ref, acc_ref, carry_ref):
    tk = tq = ATT_TILE
    d = HEAD_DIM
    seg = tk // SUBLANES
    qi = pl.program_id(2)
    qs = _split_halves(q_ref)
    n_heads = len(qs)
    row = lax.broadcasted_iota(jnp.int32, (tk, tq), 0)
    col = lax.broadcasted_iota(jnp.int32, (tk, tq), 1)
    diag_mask = (row % SUBLANES) * seg + row // SUBLANES < col
    rid = lax.broadcasted_iota(jnp.int32, (SUBLANES, tq), 0)

    def scores(kb, hh):
        pair = slice((hh // 2) * LANES, (hh // 2 + 1) * LANES)
        return lax.dot_general(k_ref[0, kb, :, pair], qs[hh], _NT, preferred_element_type=F32)

    def weights(z, carry, masked):
        cost = jnp.maximum(z, 0.0) + jnp.log(1.0 + jnp.exp(-jnp.abs(z)))
        if masked:
            cost = jnp.where(diag_mask, cost, 0.0)
        run = jnp.zeros((SUBLANES, tq), F32)
        incl = [None] * seg
        for a in reversed(range(seg)):
            run = run + cost[a * SUBLANES:(a + 1) * SUBLANES]
            incl[a] = run
        offset = jnp.zeros((SUBLANES, tq), F32)
        for r in range(1, SUBLANES):
            offset = offset + jnp.where(rid < r, jnp.broadcast_to(run[r:r + 1], (SUBLANES, tq)), 0.0)
        base = offset if carry is None else offset + carry
        w = jnp.concatenate(
            [jnp.exp((z[a * SUBLANES:(a + 1) * SUBLANES] - incl[a]) - base) for a in range(seg)],
            axis=0)
        if masked:
            w = jnp.where(diag_mask, w, 0.0)
        return w.astype(BF16), jnp.sum(run, axis=0, keepdims=True)

    prev = jnp.maximum(qi - 1, 0)
    no_prev_cost = jnp.where(qi == 0, jnp.inf, 0.0).astype(F32)
    for hh in range(n_heads):
        rows = slice(hh * d, (hh + 1) * d)
        z_diag = scores(qi, hh)
        z_prev = scores(prev, hh)
        w_diag, tot_diag = weights(z_diag, None, True)
        w_prev, tot_prev = weights(z_prev, tot_diag + no_prev_cost, False)
        acc_ref[rows, :] = (jnp.dot(vt_ref[0, qi, rows, :], w_diag, preferred_element_type=F32)
                            + jnp.dot(vt_ref[0, prev, rows, :], w_prev, preferred_element_type=F32))
        carry_ref[hh] = tot_diag + tot_prev

    def not_done():
        return jnp.min(carry_ref[...]) < SB_EXIT_COST

    def cond(state):
        kb, go = state
        return jnp.logical_and(kb >= 0, go)

    def body(state):
        kb, _ = state
        for hh in range(n_heads):
            rows = slice(hh * d, (hh + 1) * d)
            w, tot = weights(scores(kb, hh), carry_ref[hh], False)
            acc_ref[rows, :] += jnp.dot(vt_ref[0, kb, rows, :], w, preferred_element_type=F32)
            carry_ref[hh] += tot
        return kb - 1, not_done()

    lax.while_loop(cond, body, (qi - 2, not_done()))
    o_ref[0] = acc_ref[...].T


def _stick_breaking(q, k, vt, batch, seq):
    t = ATT_TILE
    nk = seq // t
    db = q.shape[-1]
    width = SB_STEP_LANES
    assert db % width == 0
    q_spec, k_spec, vt_spec = _att_specs(t, t, nk, width)
    return pl.pallas_call(
        _sb_kernel,
        grid=(batch, db // width, nk),
        in_specs=[q_spec, k_spec, vt_spec],
        out_specs=q_spec,
        out_shape=jax.ShapeDtypeStruct((batch, seq, db), F32),
        scratch_shapes=[pltpu.VMEM((width, t), F32), pltpu.VMEM((width // HEAD_DIM, 1, t), F32)],
        compiler_params=_cparams(("arbitrary", "arbitrary", "arbitrary")),
        name="stick_breaking",
    )(q.reshape(batch, seq, db), k.reshape(batch, nk, t, db), vt)


def _diff_kernel(q_ref, k_ref, vt_ref, lq1_ref, lk1_ref, lq2_ref, lk2_ref, sw_ref, o_ref,
                 s_ref, p_ref, acc_ref, m_ref, *, lam_init):
    tk = ATT_TILE
    tq = q_ref.shape[1]
    nd = tq // tk
    qi = pl.program_id(2)
    qs = _split_halves(q_ref)
    n_streams = len(qs)
    row = lax.broadcasted_iota(jnp.int32, (tk, tq), 0)
    col = lax.broadcasted_iota(jnp.int32, (tk, tq), 1)
    kb0 = qi * nd
    head = lambda i: slice((i // 2) * LANES, (i // 2 + 1) * LANES)

    def scores(kb, i):
        return lax.dot_general(k_ref[0, kb, :, head(i)], qs[i], _NT, preferred_element_type=F32)

    ones_rows = jnp.ones((acc_ref.shape[1] - LANES, tk), BF16)

    def pv_and_sum(kb, i, p):
        v_aug = jnp.concatenate([vt_ref[0, kb, head(i), :], ones_rows], axis=0)
        return jnp.dot(v_aug, p, preferred_element_type=F32)

    first_prev = jnp.maximum(kb0 - 1, 0)
    for i in range(n_streams):
        s_d = [jnp.where(row + j * tk <= col, scores(kb0 + j, i), -jnp.inf) for j in range(nd)]
        m = functools.reduce(jnp.maximum, [jnp.max(s, axis=0, keepdims=True) for s in s_d])
        p_d = [jnp.exp2(s - m).astype(BF16) for s in s_d]
        m_ref[i] = m
        acc = jnp.zeros(acc_ref.shape[1:], F32)
        for j in range(1, nd):
            acc = acc + pv_and_sum(kb0 + j, i, p_d[j])
        acc_ref[i] = acc
        p_ref[i] = p_d[0]
        s_ref[i] = scores(first_prev, i)

    def body(it, _):
        kb = kb0 - 1 - it
        nxt = jnp.maximum(kb - 1, 0)
        for i in range(n_streams):
            pv = pv_and_sum(kb + 1, i, p_ref[i])
            s_next = scores(nxt, i)
            s = s_ref[i]
            m_prev = m_ref[i]
            m_new = jnp.maximum(m_prev, jnp.max(s, axis=0, keepdims=True))
            alpha = jnp.exp2(m_prev - m_new)
            acc_ref[i] = alpha * (acc_ref[i] + pv)
            m_ref[i] = m_new
            p_ref[i] = jnp.exp2(s - m_new).astype(BF16)
            s_ref[i] = s_next
        return 0

    lax.fori_loop(0, kb0, body, 0)
    outs = []
    for i in range(n_streams):
        full = acc_ref[i] + pv_and_sum(0, i, p_ref[i])
        outs.append(full[:LANES] / full[LANES:LANES + 1])

    lam = (jnp.exp(jnp.sum(lq1_ref[...] * lk1_ref[...], axis=-1, keepdims=True))
           - jnp.exp(jnp.sum(lq2_ref[...] * lk2_ref[...], axis=-1, keepdims=True)) + lam_init)
    for h in range(n_streams // 2):
        o = outs[2 * h] - lam * outs[2 * h + 1]
        ms = jnp.mean(o * o, axis=0, keepdims=True)
        o_ref[0, :, h * LANES:(h + 1) * LANES] = (
            o * lax.rsqrt(ms + NORM_EPS) * sw_ref[...] * (1.0 - lam_init)).T


def _diff_attention(q, k, vt, lq1, lk1, lq2, lk2, subln_w, lam_init, batch, seq):
    t = ATT_TILE
    nk = seq // t
    db = q.shape[-1]
    d = HEAD_DIM
    dv = 2 * d
    assert dv == LANES
    tq = DIFF_Q_TILE
    assert seq % tq == 0 and tq % t == 0
    sw = jnp.broadcast_to(subln_w.astype(F32)[:, None], (dv, tq))
    vec = lambda a: a.astype(F32).reshape(1, d)
    kern = functools.partial(_diff_kernel, lam_init=lam_init)
    small = pl.BlockSpec((1, d), lambda bi, hi, qi: (0, 0))
    width = DIFF_STEP_LANES
    ns = 2 * (width // dv)
    assert db % width == 0
    q_spec, k_spec, vt_spec = _att_specs(tq, t, nk, width)
    return pl.pallas_call(
        kern,
        grid=(batch, db // width, seq // tq),
        in_specs=[q_spec, k_spec, vt_spec, small, small, small, small,
                  pl.BlockSpec((dv, tq), lambda bi, hi, qi: (0, 0))],
        out_specs=q_spec,
        out_shape=jax.ShapeDtypeStruct((batch, seq, db), F32),
        scratch_shapes=[pltpu.VMEM((ns, t, tq), F32), pltpu.VMEM((ns, t, tq), BF16),
                        pltpu.VMEM((ns, dv + BF16_SUBLANES, tq), F32), pltpu.VMEM((ns, 1, tq), F32)],
        compiler_params=_cparams(("arbitrary", "arbitrary", "arbitrary")),
        name="diff_attention",
    )(q.reshape(batch, seq, db), k.reshape(batch, nk, t, db), vt,
      vec(lq1), vec(lk1), vec(lq2), vec(lk2), sw)


def _gelu_tanh(x):
    return 0.5 * x * (1.0 + jnp.tanh(math.sqrt(2.0 / math.pi) * (x + 0.044715 * (x * x * x))))


def _sigmoid(x):
    return 0.5 * jnp.tanh(0.5 * x) + 0.5


def _silu(x):
    hx = 0.5 * x
    return hx * jnp.tanh(hx) + hx


def _merge_kernel(x_ref, y0_ref, y1_ref, y2_ref, y3_ref, osb_ref, odf_ref, g_ssm_ref, g_sb_ref,
                  g_df_ref, ml0_ref, ml1_ref, ml2_ref, wglu_ref, bglu_ref, bm_ref, wbr_ref, wout_ref,
                  fw_ref, o_ref, *, final_norm):
    d = x_ref.shape[-1]
    y = jnp.concatenate([r[0] for r in (y0_ref, y1_ref, y2_ref, y3_ref)], axis=1)
    y = _gelu_tanh(y)
    glu = jnp.dot(y.astype(BF16), wglu_ref[...], preferred_element_type=F32) + bglu_ref[...]
    branches = (y * _sigmoid(glu) * _silu(g_ssm_ref[...].astype(F32)),
                osb_ref[...] * _silu(g_sb_ref[...].astype(F32)),
                odf_ref[...] * _silu(g_df_ref[...].astype(F32)))
    logits = (ml0_ref, ml1_ref, ml2_ref)
    merged = None
    for n in range(3):
        gate = _sigmoid(logits[n][...].astype(F32) + bm_ref[:, n * d:(n + 1) * d])
        term = gate * jnp.dot(branches[n].astype(BF16), wbr_ref[n], preferred_element_type=F32)
        merged = term if merged is None else merged + term
    out = x_ref[...] + jnp.dot(merged.astype(BF16), wout_ref[...], preferred_element_type=F32)
    if final_norm:
        ms = jnp.mean(out * out, axis=-1, keepdims=True)
        out = out * lax.rsqrt(ms + NORM_EPS) * fw_ref[...]
    o_ref[...] = out


def _merge(x2d, yj, o_sb, o_df, rest, w_glu, b_glu, b_merge, w_branch, w_out, final_w,
           final_norm, *, tm=512):
    t, d = x2d.shape
    db = d // 2
    assert t % tm == 0 and yj.shape[0] == 4
    row = lambda i: (i, 0)
    const2 = lambda i: (0, 0)
    gate_tile0 = (3 * d) // db
    y_spec = lambda j: pl.BlockSpec((1, tm, LANES), lambda i: (j, i, 0))
    kern = functools.partial(_merge_kernel, final_norm=final_norm)
    return pl.pallas_call(
        kern,
        grid=(t // tm,),
        in_specs=[
            pl.BlockSpec((tm, d), row),
            y_spec(0), y_spec(1), y_spec(2), y_spec(3),
            pl.BlockSpec((tm, db), row),
            pl.BlockSpec((tm, db), row),
            pl.BlockSpec((tm, db), lambda i: (i, gate_tile0)),
            pl.BlockSpec((tm, db), lambda i: (i, gate_tile0 + 1)),
            pl.BlockSpec((tm, db), lambda i: (i, gate_tile0 + 2)),
            pl.BlockSpec((tm, d), lambda i: (i, 0)),
            pl.BlockSpec((tm, d), lambda i: (i, 1)),
            pl.BlockSpec((tm, d), lambda i: (i, 2)),
            pl.BlockSpec((db, db), const2),
            pl.BlockSpec((1, db), const2),
            pl.BlockSpec((1, 3 * d), const2),
            pl.BlockSpec((3, db, d), lambda i: (0, 0, 0)),
            pl.BlockSpec((d, d), const2),
            pl.BlockSpec((1, d), const2),
        ],
        out_specs=pl.BlockSpec((tm, d), row),
        out_shape=jax.ShapeDtypeStruct((t, d), F32),
        compiler_params=_cparams(("arbitrary",)),
        name="merge_out",
    )(x2d, yj, yj, yj, yj, o_sb, o_df, rest, rest, rest, rest, rest, rest,
      w_glu, b_glu.reshape(1, db).astype(F32), b_merge.reshape(1, 3 * d).astype(F32),
      w_branch, w_out, final_w.reshape(1, d).astype(F32))


def _layer(x, layer_idx, cos_t, sin_t, norm_w, w_in, b_merge, a_re, a_im, log_dt, b_re, b_im,
           c_re, c_im, d_skip, w_glu, b_glu, lq1, lk1, lq2, lk2, subln_w, w_branch, w_out,
           final_w, final_norm):
    mats = _ssm_matrices(a_re, a_im, log_dt, b_re, b_im, c_re, c_im, d_skip)
    return _layer_core(x, layer_idx, 0, cos_t, sin_t, norm_w, w_in[None].astype(BF16), b_merge, mats,
                       w_glu.astype(BF16), b_glu, lq1, lk1, lq2, lk2, subln_w, w_branch.astype(BF16),
                       w_out.astype(BF16), final_w, final_norm)


def _layer_core(x, layer_idx, w_slot, cos_t, sin_t, norm_w, w_bf, b_merge, mats, w_glu, b_glu,
                lq1, lk1, lq2, lk2, subln_w, w_branch, w_out, final_w, final_norm):
    b, s, d = x.shape
    db = d // 2
    x2d = x.reshape(b * s, d)
    rest, uj, sbq, sbk, dfq, dfk, sbvt, dfvt = _in_proj(x2d, norm_w, w_bf, w_slot, cos_t, sin_t, b, s)
    yj = _ssm(uj, mats, b, s)
    o_sb = _stick_breaking(sbq, sbk, sbvt, b, s)
    lam_init = 0.8 - 0.6 * math.exp(-0.3 * layer_idx)
    o_df = _diff_attention(dfq, dfk, dfvt, lq1, lk1, lq2, lk2, subln_w, lam_init, b, s)

    out = _merge(x2d, yj, o_sb.reshape(b * s, db), o_df.reshape(b * s, db),
                 rest, w_glu, b_glu, b_merge, w_branch, w_out, final_w, final_norm)
    return out.reshape(b, s, d)


def kernel(x, norm_w, w_in, b_merge, ssm_a_re, ssm_a_im, ssm_log_dt, ssm_b_re, ssm_b_im, ssm_c_re,
           ssm_c_im, ssm_d, ssm_w_glu, ssm_b_glu, diff_lq1, diff_lk1, diff_lq2, diff_lk2,
           diff_subln_w, w_branch, w_out, final_norm_w):
    depth = norm_w.shape[0]
    seq = x.shape[1]
    db = x.shape[2] // 2
    cos_t, sin_t = _rope_tables(seq, db)
    w_bf = w_in.astype(BF16)
    mats = jax.vmap(_ssm_matrices)(ssm_a_re, ssm_a_im, ssm_log_dt, ssm_b_re, ssm_b_im, ssm_c_re,
                                   ssm_c_im, ssm_d)
    w_glu, w_br, w_o = ssm_w_glu.astype(BF16), w_branch.astype(BF16), w_out.astype(BF16)
    for i in range(depth):
        x = _layer_core(x, i, i, cos_t, sin_t, norm_w[i], w_bf, b_merge[i],
                        [m[i] for m in mats], w_glu[i], ssm_b_glu[i], diff_lq1[i], diff_lk1[i],
                        diff_lq2[i], diff_lk2[i], diff_subln_w[i], w_br[i], w_o[i], final_norm_w,
                        i == depth - 1)
    return x
```
